```python
import math
import jax, jax.numpy as jnp
from jax import lax
import numpy as np

D_MODEL = 1024
BATCH = 1
SEQ = 16384
DEPTH = 1

DN_HEADS = 4
DN_HEAD_DIM = 128
DN_WIDTH = DN_HEADS * DN_HEAD_DIM
CONV_WIDTH = 4
DN_CHUNK = 64
DA_HEADS = 4
DA_HEAD_DIM = 64
DA_WIDTH = DA_HEADS * 2 * DA_HEAD_DIM
Q_BLOCK = 128
ROPE_THETA = 10000.0
N_EXPERTS = 32
TOP_K = 4
D_EXPERT = D_MODEL
SWIGLU_LIMIT = 7.0
SWIGLU_ALPHA = 1.702
MOE_BLOCK = 256
N_MODS = 6
IN_SIZES = (DN_WIDTH, DN_WIDTH, DN_WIDTH, DN_WIDTH, DN_HEADS, DN_HEADS,
            DA_WIDTH, DA_WIDTH, DA_WIDTH, D_MODEL, D_MODEL)
IN_OFFSETS = tuple(sum(IN_SIZES[:i + 1]) for i in range(len(IN_SIZES) - 1))
IN_COLS = sum(IN_SIZES)

kernel_name = 'hybrid_deltanet_diffattn_moe_deepnorm_adaln'


def _deepnorm_alpha():
    return (2.0 * DEPTH) ** 0.25


def _deepnorm_beta():
    return (8.0 * DEPTH) ** -0.25


def _layer_norm(x, g, b, eps=1e-5):
    xf = x.astype(jnp.float32)
    mu = jnp.mean(xf, -1, keepdims=True)
    var = jnp.mean(jnp.square(xf - mu), -1, keepdims=True)
    return ((xf - mu) * lax.rsqrt(var + eps) * g + b).astype(x.dtype)


def _rms_norm(x, w, eps=1e-5):
    xf = x.astype(jnp.float32)
    return (xf * lax.rsqrt(jnp.mean(xf * xf, -1, keepdims=True) + eps) * w).astype(x.dtype)


def _l2_normalize(x, eps=1e-6):
    xf = x.astype(jnp.float32)
    return (xf * lax.rsqrt(jnp.sum(xf * xf, -1, keepdims=True) + eps)).astype(x.dtype)


def _causal_depthwise_conv(x, w):
    k, ch = w.shape
    return lax.conv_general_dilated(x, w[:, None, :].astype(x.dtype), window_strides=(1,),
                                    padding=[(k - 1, 0)], dimension_numbers=('NWC', 'WIO', 'NWC'),
                                    feature_group_count=ch)


def _rope(x, positions):
    d = x.shape[-1]
    half = d // 2
    inv_freq = ROPE_THETA ** (-jnp.arange(half, dtype=jnp.float32) / half)
    ang = positions.astype(jnp.float32)[..., None] * inv_freq
    ang = ang.reshape(ang.shape[:2] + (1,) * (x.ndim - 3) + (half,))
    cos, sin = jnp.cos(ang), jnp.sin(ang)
    xf = x.astype(jnp.float32)
    x1, x2 = xf[..., :half], xf[..., half:]
    return jnp.concatenate([x1 * cos - x2 * sin, x2 * cos + x1 * sin], -1).astype(x.dtype)


def _gated_delta_rule(q, k, v, g, beta):
    out_dtype = v.dtype
    b, s, h, dk = q.shape
    dv = v.shape[-1]
    c = DN_CHUNK
    n = s // c

    def to_chunks(t):
        t = t.astype(jnp.float32).reshape((b, n, c, h) + t.shape[3:])
        return jnp.moveaxis(t, 3, 1)

    q, k, v, g, beta = (to_chunks(t) for t in (q, k, v, g, beta))
    cum_g = jnp.cumsum(g, axis=-1)
    idx = jnp.arange(c)
    strict = idx[:, None] > idx[None, :]
    incl = idx[:, None] >= idx[None, :]
    decay = jnp.exp(jnp.where(incl, cum_g[..., :, None] - cum_g[..., None, :], -jnp.inf))
    kk = jnp.einsum('bhnid,bhnjd->bhnij', k, k)
    a_mat = jnp.where(strict, beta[..., None] * decay * kk, 0.0)
    lower = jnp.eye(c, dtype=jnp.float32) + a_mat
    rhs = jnp.concatenate([beta[..., None] * v, (beta * jnp.exp(cum_g))[..., None] * k], -1)
    sol = lax.linalg.triangular_solve(lower, rhs, left_side=True, lower=True)
    u_base, w_mat = sol[..., :dv], sol[..., dv:]
    qk = jnp.einsum('bhnid,bhnjd->bhnij', q, k) * decay
    q_dec = q * jnp.exp(cum_g)[..., None]
    k_dec = k * jnp.exp(cum_g[..., -1:] - cum_g)[..., None]
    chunk_decay = jnp.exp(cum_g[..., -1])

    def step(state, inp):
        u_b, w_c, qk_c, qd_c, kd_c, cd_c = inp
        u = u_b - jnp.einsum('bhcd,bhde->bhce', w_c, state)
        o = jnp.einsum('bhcd,bhde->bhce', qd_c, state) + jnp.einsum('bhij,bhje->bhie', qk_c, u)
        state = state * cd_c[..., None, None] + jnp.einsum('bhcd,bhce->bhde', kd_c, u)
        return state, o

    xs = tuple(jnp.moveaxis(t, 2, 0) for t in (u_base, w_mat, qk, q_dec, k_dec, chunk_decay))
    state0 = jnp.zeros((b, h, dk, dv), jnp.float32)
    _, o = lax.scan(step, state0, xs)
    o = jnp.transpose(o, (1, 0, 3, 2, 4)).reshape(b, s, h, dv)
    return o.astype(out_dtype)


def _diff_attention(q, k, v, lam):
    b, s, h, _, d = q.shape
    nblk = s // Q_BLOCK
    scale = d ** -0.5
    kpos = jnp.arange(s)

    def one_block(i):
        start = i * Q_BLOCK
        qb = lax.dynamic_slice_in_dim(q, start, Q_BLOCK, axis=1)
        sc = jnp.einsum('bqhmd,bkhmd->bhmqk', qb, k, preferred_element_type=jnp.float32) * scale
        qpos = start + jnp.arange(Q_BLOCK)
        mask = kpos[None, :] <= qpos[:, None]
        p = jax.nn.softmax(jnp.where(mask, sc, -jnp.inf), axis=-1)
        a = p[:, :, 0] - lam * p[:, :, 1]
        return jnp.einsum('bhqk,bkhe->bqhe', a.astype(v.dtype), v)

    o = lax.map(one_block, jnp.arange(nblk))
    return jnp.moveaxis(o, 0, 1).reshape(b, s, h, 2 * d)


def _clamped_swiglu(gu):
    x_glu, x_lin = gu[..., ::2], gu[..., 1::2]
    x_glu = jnp.minimum(x_glu, SWIGLU_LIMIT)
    x_lin = jnp.clip(x_lin, -SWIGLU_LIMIT, SWIGLU_LIMIT)
    return x_glu * jax.nn.sigmoid(SWIGLU_ALPHA * x_glu) * (x_lin + 1.0)


def _moe(h, w_router, b_router, w_gate_up, b_gate_up, w_down, b_down):
    b, s, d = h.shape
    m = b * s
    hf = h.reshape(m, d)
    logits = (hf @ w_router + b_router).astype(jnp.float32)
    top_val, top_idx = lax.top_k(logits, TOP_K)
    top_w = jax.nn.softmax(top_val, axis=-1)
    n_assign = m * TOP_K
    expert_flat = top_idx.reshape(-1).astype(jnp.int32)
    token_flat = jnp.arange(n_assign, dtype=jnp.int32) // TOP_K
    weight_flat = top_w.reshape(-1)
    order = jnp.argsort(expert_flat)
    e_sorted = expert_flat[order]
    counts = jnp.bincount(expert_flat, length=N_EXPERTS).astype(jnp.int32)
    group_start = jnp.cumsum(counts) - counts
    padded = (counts + MOE_BLOCK - 1) // MOE_BLOCK * MOE_BLOCK
    padded_end = jnp.cumsum(padded)
    padded_start = padded_end - padded
    rank = jnp.arange(n_assign, dtype=jnp.int32) - group_start[e_sorted]
    dest = padded_start[e_sorted] + rank
    n_blocks = -(-n_assign // MOE_BLOCK) + N_EXPERTS
    n_rows = n_blocks * MOE_BLOCK
    row_token = jnp.full((n_rows,), m, jnp.int32).at[dest].set(token_flat[order])
    row_weight = jnp.zeros((n_rows,), jnp.float32).at[dest].set(weight_flat[order])
    block_start = jnp.arange(n_blocks, dtype=jnp.int32) * MOE_BLOCK
    block_expert = jnp.minimum(jnp.searchsorted(padded_end, block_start, side='right'), N_EXPERTS - 1)
    h_pad = jnp.concatenate([hf, jnp.zeros((1, d), hf.dtype)], 0)

    def expert_block(args):
        rows, e = args
        xb = h_pad[rows]
        gu = xb @ w_gate_up[e] + b_gate_up[e]
        return _clamped_swiglu(gu) @ w_down[e] + b_down[e]

    out = lax.map(expert_block, (row_token.reshape(n_blocks, MOE_BLOCK), block_expert))
    out = out.reshape(n_rows, d) * row_weight.astype(out.dtype)[:, None]
    y = jnp.zeros((m + 1, d), out.dtype).at[row_token].add(out)[:m]
    return y.reshape(b, s, d)


def _mixer(u, positions, layer, w_in, conv_w, dn_a_log, dn_dt_bias, dn_norm_w, w_dn_proj,
           lambda_q1, lambda_k1, lambda_q2, lambda_k2, da_norm_w, w_da_proj, w_o):
    b, s, _ = u.shape
    proj = u @ w_in
    qkv_end = IN_OFFSETS[2]
    qkv = jax.nn.silu(_causal_depthwise_conv(proj[..., :qkv_end], conv_w))
    dq, dk, dv = jnp.split(qkv, 3, axis=-1)
    z, b_logit, a_in, aq, ak, av, g_dn, g_da = jnp.split(
        proj[..., qkv_end:], [o - qkv_end for o in IN_OFFSETS[3:]], axis=-1)

    dq = _l2_normalize(dq.reshape(b, s, DN_HEADS, DN_HEAD_DIM)) * (DN_HEAD_DIM ** -0.5)
    dk = _l2_normalize(dk.reshape(b, s, DN_HEADS, DN_HEAD_DIM))
    dv = dv.reshape(b, s, DN_HEADS, DN_HEAD_DIM)
    beta = jax.nn.sigmoid(b_logit.astype(jnp.float32))
    g = -jnp.exp(dn_a_log.astype(jnp.float32)) * jax.nn.softplus(
        a_in.astype(jnp.float32) + dn_dt_bias.astype(jnp.float32))
    o_dn = _gated_delta_rule(dq, dk, dv, g, beta)
    o_dn = _rms_norm(o_dn, dn_norm_w) * jax.nn.silu(z.reshape(b, s, DN_HEADS, DN_HEAD_DIM))
    y_dn = o_dn.reshape(b, s, DN_WIDTH) @ w_dn_proj

    lam_init = 0.8 - 0.6 * math.exp(-0.3 * layer)
    lam = (jnp.exp(jnp.sum(lambda_q1.astype(jnp.float32) * lambda_k1.astype(jnp.float32)))
           - jnp.exp(jnp.sum(lambda_q2.astype(jnp.float32) * lambda_k2.astype(jnp.float32))) + lam_init)
    aq = _rope(aq.reshape(b, s, DA_HEADS, 2, DA_HEAD_DIM), positions)
    ak = _rope(ak.reshape(b, s, DA_HEADS, 2, DA_HEAD_DIM), positions)
    av = av.reshape(b, s, DA_HEADS, 2 * DA_HEAD_DIM)
    o_da = _diff_attention(aq, ak, av, lam)
    o_da = _rms_norm(o_da, da_norm_w) * (1.0 - lam_init)
    y_da = o_da.reshape(b, s, DA_WIDTH) @ w_da_proj

    merged = jax.nn.sigmoid(g_dn) * y_dn + jax.nn.sigmoid(g_da) * y_da
    return merged @ w_o


def setup_inputs(seed: int = 0) -> dict:
    key = jax.random.key(seed)
    keys = jax.random.split(key, 48)
    counter = [0]

    def nxt():
        k = keys[counter[0]]
        counter[0] += 1
        return k

    def nrm(shape, scale):
        return jax.random.normal(nxt(), shape, jnp.float32) * scale

    L, D = DEPTH, D_MODEL
    beta = _deepnorm_beta()
    x = nrm((BATCH, SEQ, D), 1.0)
    c = nrm((BATCH, D), 1.0)
    positions = jnp.broadcast_to(jnp.arange(SEQ, dtype=jnp.int32), (BATCH, SEQ))
    w_ada = nrm((L, D, N_MODS * D), D ** -0.5)
    b_ada = nrm((L, N_MODS * D), 0.02)
    in_gains = (1.0, 1.0, beta, 1.0, 1.0, 1.0, 1.0, 1.0, beta, 1.0, 1.0)
    w_in = jnp.concatenate([nrm((L, D, n), gn * D ** -0.5) for n, gn in zip(IN_SIZES, in_gains)], -1)
    conv_w = nrm((L, CONV_WIDTH, 3 * DN_WIDTH), CONV_WIDTH ** -0.5)
    dn_a_log = jnp.log(jax.random.uniform(nxt(), (L, DN_HEADS), jnp.float32, 1.0, 16.0))
    dt = jnp.exp(jax.random.uniform(nxt(), (L, DN_HEADS), jnp.float32)
                 * (math.log(0.1) - math.log(0.001)) + math.log(0.001))
    dn_dt_bias = dt + jnp.log(-jnp.expm1(-dt))
    dn_norm_w = 1.0 + nrm((L, DN_HEAD_DIM), 0.02)
    w_dn_proj = nrm((L, DN_WIDTH, D), DN_WIDTH ** -0.5)
    lambda_q1 = nrm((L, DA_HEAD_DIM), 0.1)
    lambda_k1 = nrm((L, DA_HEAD_DIM), 0.1)
    lambda_q2 = nrm((L, DA_HEAD_DIM), 0.1)
    lambda_k2 = nrm((L, DA_HEAD_DIM), 0.1)
    da_norm_w = 1.0 + nrm((L, 2 * DA_HEAD_DIM), 0.02)
    w_da_proj = nrm((L, DA_WIDTH, D), DA_WIDTH ** -0.5)
    w_o = nrm((L, D, D), beta * D ** -0.5)
    ln1_g = 1.0 + nrm((L, D), 0.02)
    ln1_b = nrm((L, D), 0.02)
    w_router = nrm((L, D, N_EXPERTS), D ** -0.5)
    b_router = nrm((L, N_EXPERTS), 0.01)
    w_gate_up = nrm((L, N_EXPERTS, D, 2 * D_EXPERT), beta * D ** -0.5)
    b_gate_up = nrm((L, N_EXPERTS, 2 * D_EXPERT), 0.02)
    w_down = nrm((L, N_EXPERTS, D_EXPERT, D), beta * D_EXPERT ** -0.5)
    b_down = nrm((L, N_EXPERTS, D), 0.02)
    ln2_g = 1.0 + nrm((L, D), 0.02)
    ln2_b = nrm((L, D), 0.02)
    return {'x': x, 'c': c, 'positions': positions, 'w_ada': w_ada, 'b_ada': b_ada, 'w_in': w_in,
            'conv_w': conv_w, 'dn_a_log': dn_a_log, 'dn_dt_bias': dn_dt_bias, 'dn_norm_w': dn_norm_w,
            'w_dn_proj': w_dn_proj, 'lambda_q1': lambda_q1, 'lambda_k1': lambda_k1,
            'lambda_q2': lambda_q2, 'lambda_k2': lambda_k2, 'da_norm_w': da_norm_w,
            'w_da_proj': w_da_proj, 'w_o': w_o, 'ln1_g': ln1_g, 'ln1_b': ln1_b,
            'w_router': w_router, 'b_router': b_router, 'w_gate_up': w_gate_up,
            'b_gate_up': b_gate_up, 'w_down': w_down, 'b_down': b_down, 'ln2_g': ln2_g, 'ln2_b': ln2_b}


def reference(x, c, positions, w_ada, b_ada, w_in, conv_w, dn_a_log, dn_dt_bias, dn_norm_w,
              w_dn_proj, lambda_q1, lambda_k1, lambda_q2, lambda_k2, da_norm_w, w_da_proj, w_o,
              ln1_g, ln1_b, w_router, b_router, w_gate_up, b_gate_up, w_down, b_down, ln2_g, ln2_b):
    alpha = _deepnorm_alpha()
    silu_c = jax.nn.silu(c)
    for l in range(DEPTH):
        mods = (silu_c @ w_ada[l] + b_ada[l])[:, None, :]
        shift_m, scale_m, gate_m, shift_f, scale_f, gate_f = jnp.split(mods, N_MODS, axis=-1)
        u = x * (1.0 + scale_m) + shift_m
        mix = _mixer(u, positions, l, w_in[l], conv_w[l], dn_a_log[l], dn_dt_bias[l], dn_norm_w[l],
                     w_dn_proj[l], lambda_q1[l], lambda_k1[l], lambda_q2[l], lambda_k2[l],
                     da_norm_w[l], w_da_proj[l], w_o[l])
        x = _layer_norm(alpha * x + gate_m * mix, ln1_g[l], ln1_b[l])
        u = x * (1.0 + scale_f) + shift_f
        ffn = _moe(u, w_router[l], b_router[l], w_gate_up[l], b_gate_up[l], w_down[l], b_down[l])
        x = _layer_norm(alpha * x + gate_f * ffn, ln2_g[l], ln2_b[l])
    return x
```

```python
import functools
import math

import numpy as np
import jax
import jax.numpy as jnp
from jax import lax
from jax.experimental import pallas as pl
from jax.experimental.pallas import tpu as pltpu

F32 = jnp.float32
BF16 = jnp.bfloat16
I32 = jnp.int32
HIGHEST = lax.Precision.HIGHEST

DN_HEADS = 4
DN_HEAD_DIM = 128
DN_WIDTH = DN_HEADS * DN_HEAD_DIM
CONV_WIDTH = 4
DN_CHUNK = 64
DA_HEADS = 4
DA_HEAD_DIM = 64
DA_VDIM = 2 * DA_HEAD_DIM
DA_WIDTH = DA_HEADS * DA_VDIM
ROPE_THETA = 10000.0
N_EXPERTS = 32
TOP_K = 4
SWIGLU_LIMIT = 7.0
SWIGLU_ALPHA = 1.702
LN_EPS = 1e-5
L2_EPS = 1e-6

LANES = 128
SUBLANES = 8
VMEM_LIMIT = 56 * 1024 * 1024

COL_GDN, COL_GDA, COL_DQ, COL_DK, COL_DV, COL_Z, COL_AQ, COL_AK, COL_AV, COL_BA = (
    0, 1024, 2048, 2560, 3072, 3584, 4096, 4608, 5120, 5632)
NP_COLS = 5760

MOE_TM = 256


def _sigmoid(x):
    return 1.0 / (1.0 + jnp.exp(-x))


def _silu(x):
    return x * _sigmoid(x)


def _bdot(a, b):
    return jnp.dot(a.astype(BF16), b.astype(BF16), preferred_element_type=F32)


def _bdot_nt(a, b):
    return lax.dot_general(a.astype(BF16), b.astype(BF16), (((1,), (1,)), ((), ())),
                           preferred_element_type=F32)


def _bdot_tn(a, b):
    return lax.dot_general(a.astype(BF16), b.astype(BF16), (((0,), (0,)), ((), ())),
                           preferred_element_type=F32)


def _layer_norm_rows(r, g, b):
    mu = jnp.mean(r, axis=-1, keepdims=True)
    d = r - mu
    var = jnp.mean(d * d, axis=-1, keepdims=True)
    return d * lax.rsqrt(var + LN_EPS) * g + b


def _mods_body(c_ref, w_ref, b_ref, o_ref):
    c = c_ref[...]
    o_ref[...] = jnp.dot(_silu(c), w_ref[...], preferred_element_type=F32,
                         precision=HIGHEST) + b_ref[...]


def _mods(c8, w_ada, b_ada):
    d, n = w_ada.shape
    tn = 1536
    return pl.pallas_call(
        _mods_body,
        out_shape=jax.ShapeDtypeStruct((SUBLANES, n), F32),
        grid=(n // tn,),
        in_specs=[pl.BlockSpec((SUBLANES, d), lambda j: (0, 0)),
                  pl.BlockSpec((d, tn), lambda j: (0, j)),
                  pl.BlockSpec((1, tn), lambda j: (0, j))],
        out_specs=pl.BlockSpec((SUBLANES, tn), lambda j: (0, j)),
        compiler_params=pltpu.CompilerParams(dimension_semantics=("arbitrary",),
                                             vmem_limit_bytes=VMEM_LIMIT),
        name="mods",
    )(c8, w_ada, b_ada)


def _inproj_body(x_ref, shift_ref, scale_ref, w_ref, o_ref, u_ref):
    @pl.when(pl.program_id(1) == 0)
    def _():
        u = x_ref[...] * (1.0 + scale_ref[0:1, :]) + shift_ref[0:1, :]
        u_ref[...] = u.astype(BF16)

    o_ref[...] = jnp.dot(u_ref[...], w_ref[...], preferred_element_type=F32)


def _inproj(x2, mods, w_re):
    s, d = x2.shape
    tm = min(s, 1024)
    tn = 1152
    return pl.pallas_call(
        _inproj_body,
        out_shape=jax.ShapeDtypeStruct((s, NP_COLS), F32),
        grid=(s // tm, NP_COLS // tn),
        in_specs=[pl.BlockSpec((tm, d), lambda i, j: (i, 0)),
                  pl.BlockSpec((SUBLANES, d), lambda i, j: (0, 0)),
                  pl.BlockSpec((SUBLANES, d), lambda i, j: (0, 1)),
                  pl.BlockSpec((d, tn), lambda i, j: (0, j))],
        out_specs=pl.BlockSpec((tm, tn), lambda i, j: (i, j)),
        scratch_shapes=[pltpu.VMEM((tm, d), BF16)],
        compiler_params=pltpu.CompilerParams(dimension_semantics=("arbitrary", "arbitrary"),
                                             vmem_limit_bytes=VMEM_LIMIT),
        name="inproj",
    )(x2, mods, mods, w_re)


ATT_T = 512


def _attnprep_body(pos_ref, invf_ref, sgn_ref, aq_ref, ak_ref, av_ref,
                   q1t_ref, q2t_ref, k_ref, vt_ref):
    ang = pos_ref[...].astype(F32) * invf_ref[...]
    cos = jnp.cos(ang)
    sin = jnp.sin(ang) * sgn_ref[...]
    t = ang.shape[0]
    lane = lax.broadcasted_iota(I32, (t, LANES), 1)
    first_half = (lane & (DA_HEAD_DIM - 1)) < (DA_HEAD_DIM // 2)
    low_map = lane < DA_HEAD_DIM

    def rope(x):
        swapped = jnp.where(first_half, pltpu.roll(x, LANES - DA_HEAD_DIM // 2, 1),
                            pltpu.roll(x, DA_HEAD_DIM // 2, 1))
        return x * cos + swapped * sin

    scale = DA_HEAD_DIM ** -0.5
    for h in range(DA_HEADS):
        cols = slice(h * LANES, (h + 1) * LANES)
        q = rope(aq_ref[:, cols]) * scale
        q1t_ref[h] = jnp.where(low_map, q, 0.0).T.astype(BF16)
        q2t_ref[h] = jnp.where(low_map, 0.0, q).T.astype(BF16)
        k_ref[:, cols] = rope(ak_ref[:, cols]).astype(BF16)
        vt_ref[h, 0] = av_ref[:, cols].T.astype(BF16)


def _attnprep(pos_col, invf, sgn, proj):
    s = proj.shape[0]
    t = ATT_T
    nt = s // t
    w = DA_WIDTH
    return pl.pallas_call(
        _attnprep_body,
        out_shape=(jax.ShapeDtypeStruct((DA_HEADS, LANES, s), BF16),
                   jax.ShapeDtypeStruct((DA_HEADS, LANES, s), BF16),
                   jax.ShapeDtypeStruct((s, w), BF16),
                   jax.ShapeDtypeStruct((DA_HEADS, nt, LANES, t), BF16)),
        grid=(nt,),
        in_specs=[pl.BlockSpec((t, 1), lambda i: (i, 0)),
                  pl.BlockSpec((1, LANES), lambda i: (0, 0)),
                  pl.BlockSpec((1, LANES), lambda i: (0, 0)),
                  pl.BlockSpec((t, w), lambda i: (i, COL_AQ // w)),
                  pl.BlockSpec((t, w), lambda i: (i, COL_AK // w)),
                  pl.BlockSpec((t, w), lambda i: (i, COL_AV // w))],
        out_specs=(pl.BlockSpec((DA_HEADS, LANES, t), lambda i: (0, 0, i)),
                   pl.BlockSpec((DA_HEADS, LANES, t), lambda i: (0, 0, i)),
                   pl.BlockSpec((t, w), lambda i: (i, 0)),
                   pl.BlockSpec((DA_HEADS, 1, LANES, t), lambda i: (0, i, 0, 0))),
        compiler_params=pltpu.CompilerParams(dimension_semantics=("arbitrary",),
                                             vmem_limit_bytes=VMEM_LIMIT),
        name="attnprep",
    )(pos_col, invf, sgn, proj, proj, proj)


def _attn_body(lam_init, q1t_ref, q2t_ref, k_ref, vt_ref, lq1_ref, lk1_ref, lq2_ref, lk2_ref,
               nw_ref, o_ref, acc_ref):
    t = ATT_T
    i = pl.program_id(1)
    qts = (q1t_ref[...], q2t_ref[...])
    neg_inf = jnp.full((1, t), -jnp.inf, F32)
    zeros = jnp.zeros((1, t), F32)
    acc_ref[...] = jnp.zeros_like(acc_ref)

    def tile(j, carry, masked):
        start = pl.multiple_of(j * t, t)
        kt = k_ref[pl.ds(start, t), :]
        vt = vt_ref[j]
        if masked:
            row = lax.broadcasted_iota(I32, (t, t), 0)
            col = lax.broadcasted_iota(I32, (t, t), 1)
            keep = row <= col
        out = []
        for m in range(2):
            m_old, l_old = carry[2 * m], carry[2 * m + 1]
            st = jnp.dot(kt, qts[m], preferred_element_type=F32)
            if masked:
                st = jnp.where(keep, st, -jnp.inf)
            m_new = jnp.maximum(m_old, jnp.max(st, axis=0, keepdims=True))
            alpha = jnp.exp(m_old - m_new)
            p = jnp.exp(st - m_new)
            l_new = alpha * l_old + jnp.sum(p, axis=0, keepdims=True)
            acc_ref[m] = alpha * acc_ref[m] + jnp.dot(vt, p.astype(BF16),
                                                      preferred_element_type=F32)
            out += [m_new, l_new]
        return tuple(out)

    carry = lax.fori_loop(0, i, lambda j, c: tile(j, c, False),
                          (neg_inf, zeros, neg_inf, zeros))
    _, l1, _, l2 = tile(i, carry, True)

    lam = (jnp.exp(jnp.sum(lq1_ref[...] * lk1_ref[...], axis=-1, keepdims=True))
           - jnp.exp(jnp.sum(lq2_ref[...] * lk2_ref[...], axis=-1, keepdims=True)) + lam_init)
    ot = acc_ref[0] / l1 - lam * (acc_ref[1] / l2)
    ms = jnp.mean(ot * ot, axis=0, keepdims=True)
    ot = ot * lax.rsqrt(ms + LN_EPS) * nw_ref[...] * (1.0 - lam_init)
    o_ref[...] = ot.T


def _attn(lam_init, q1t, q2t, k12, vt3, lq1, lk1, lq2, lk2, nw_col):
    s = k12.shape[0]
    t = ATT_T
    nt = s // t
    small = pl.BlockSpec((1, DA_HEAD_DIM), lambda h, i: (0, 0))
    return pl.pallas_call(
        functools.partial(_attn_body, lam_init),
        out_shape=jax.ShapeDtypeStruct((s, DA_WIDTH), F32),
        grid=(DA_HEADS, nt),
        in_specs=[pl.BlockSpec((None, LANES, t), lambda h, i: (h, 0, i)),
                  pl.BlockSpec((None, LANES, t), lambda h, i: (h, 0, i)),
                  pl.BlockSpec((s, LANES), lambda h, i: (0, h)),
                  pl.BlockSpec((None, nt, LANES, t), lambda h, i: (h, 0, 0, 0)),
                  small, small, small, small,
                  pl.BlockSpec((DA_VDIM, 1), lambda h, i: (0, 0))],
        out_specs=pl.BlockSpec((t, LANES), lambda h, i: (i, h)),
        scratch_shapes=[pltpu.VMEM((2, LANES, t), F32)],
        compiler_params=pltpu.CompilerParams(dimension_semantics=("arbitrary", "arbitrary"),
                                             vmem_limit_bytes=VMEM_LIMIT),
        name="attn",
    )(q1t, q2t, k12, vt3, lq1, lk1, lq2, lk2, nw_col)


DN_NC = 1
DN_TT = DN_NC * DN_CHUNK


def _tri_inverse(a, ii, jj, eye):
    n = jnp.where((ii >> 3) == (jj >> 3), -a, 0.0)
    n2 = _bdot(n, n)
    n4 = _bdot(n2, n2)
    p = eye + n + n2 + _bdot(n, n2)
    d = p + _bdot(p, n4)
    for sh in (3, 4, 5):
        bi, bj = ii >> sh, jj >> sh
        lo = jnp.where(((bi & 1) == 1) & (bj == bi - 1), a, 0.0)
        d = d - _bdot(_bdot(d, lo), d)
    return d


def _dn_body(q_ref, k_ref, v_ref, ba_ref, cw_ref, alog_ref, dtb_ref, o_ref, xbuf_ref, state_ref):
    tt = DN_TT
    c = DN_CHUNK

    @pl.when(pl.program_id(0) == 0)
    def _():
        xbuf_ref[:, 0:SUBLANES, :] = jnp.zeros((3, SUBLANES, DN_WIDTH), F32)
        state_ref[...] = jnp.zeros_like(state_ref)

    ys = []
    for n, ref in enumerate((q_ref, k_ref, v_ref)):
        cur = ref[...]
        xbuf_ref[n, SUBLANES:SUBLANES + tt, :] = cur
        acc = None
        for j in range(CONV_WIDTH):
            off = SUBLANES - (CONV_WIDTH - 1) + j
            term = cw_ref[j:j + 1, n * DN_WIDTH:(n + 1) * DN_WIDTH] * xbuf_ref[n, off:off + tt, :]
            acc = term if acc is None else acc + term
        xbuf_ref[n, 0:SUBLANES, :] = cur[tt - SUBLANES:tt, :]
        ys.append(_silu(acc))
    yq, yk, yv = ys

    ba = ba_ref[...]
    beta_full = _sigmoid(ba)
    g_full = -jnp.exp(alog_ref[...]) * jax.nn.softplus(ba + dtb_ref[...])

    ii = lax.broadcasted_iota(I32, (c, c), 0)
    jj = lax.broadcasted_iota(I32, (c, c), 1)
    eye = (ii == jj).astype(F32)
    tril = (ii >= jj).astype(F32)
    strict = ii > jj
    strict_f = strict.astype(F32)
    supper = (ii < jj).astype(F32)

    for ci in range(DN_NC):
        rows = slice(ci * c, (ci + 1) * c)
        gc = g_full[rows, :]
        cg = jnp.dot(tril, gc, preferred_element_type=F32, precision=HIGHEST)
        rev = jnp.dot(supper, gc, preferred_element_type=F32, precision=HIGHEST)
        for h in range(DN_HEADS):
            cols = slice(h * DN_HEAD_DIM, (h + 1) * DN_HEAD_DIM)
            gl = DN_HEADS + h
            qh, kh, vh = yq[rows, cols], yk[rows, cols], yv[rows, cols]
            qn = qh * lax.rsqrt(jnp.sum(qh * qh, axis=-1, keepdims=True) + L2_EPS) * (DN_HEAD_DIM ** -0.5)
            kn = kh * lax.rsqrt(jnp.sum(kh * kh, axis=-1, keepdims=True) + L2_EPS)
            beta = beta_full[rows, h:h + 1]
            g_col = gc[:, gl:gl + 1]
            eg = jnp.exp(cg[:, gl:gl + 1])
            erev = jnp.exp(rev[:, gl:gl + 1])
            etot = jnp.exp(cg[c - 1:c, gl:gl + 1])
            ldiff = jnp.dot(tril, g_col * strict_f, preferred_element_type=F32, precision=HIGHEST)
            decay = jnp.exp(jnp.where(ii >= jj, ldiff, -jnp.inf))
            kb = kn.astype(BF16)
            kk = _bdot_nt(kb, kb)
            a = jnp.where(strict, beta * decay * kk, 0.0)
            tinv = _tri_inverse(a, ii, jj, eye).astype(BF16)
            u_base = _bdot(tinv, beta * vh)
            w = _bdot(tinv, (beta * eg) * kn)
            qk = _bdot_nt(qn, kb) * decay
            st = state_ref[h]
            sb = st.astype(BF16)
            u = u_base - _bdot(w, sb)
            o = _bdot(qn * eg, sb) + _bdot(qk, u)
            state_ref[h] = st * etot + _bdot_tn(kn * erev, u)
            o_ref[rows, cols] = o


def _deltanet(proj, conv_w, alog_row, dtb_row):
    s = proj.shape[0]
    tt = DN_TT
    w = DN_WIDTH
    return pl.pallas_call(
        _dn_body,
        out_shape=jax.ShapeDtypeStruct((s, w), F32),
        grid=(s // tt,),
        in_specs=[pl.BlockSpec((tt, w), lambda i: (i, COL_DQ // w)),
                  pl.BlockSpec((tt, w), lambda i: (i, COL_DK // w)),
                  pl.BlockSpec((tt, w), lambda i: (i, COL_DV // w)),
                  pl.BlockSpec((tt, LANES), lambda i: (i, COL_BA // LANES)),
                  pl.BlockSpec((CONV_WIDTH, 3 * w), lambda i: (0, 0)),
                  pl.BlockSpec((1, LANES), lambda i: (0, 0)),
                  pl.BlockSpec((1, LANES), lambda i: (0, 0))],
        out_specs=pl.BlockSpec((tt, w), lambda i: (i, 0)),
        scratch_shapes=[pltpu.VMEM((3, tt + SUBLANES, w), F32),
                        pltpu.VMEM((DN_HEADS, DN_HEAD_DIM, DN_HEAD_DIM), F32)],
        compiler_params=pltpu.CompilerParams(dimension_semantics=("arbitrary",),
                                             vmem_limit_bytes=VMEM_LIMIT),
        name="deltanet",
    )(proj, proj, proj, proj, conv_w, alog_row, dtb_row)


def _post_body(alpha, x_ref, odn_ref, z_ref, oda_ref, gdn_ref, gda_ref, gate_ref, scale_ref,
               shift_ref, dnw_ref, wdn_ref, wda_ref, wo_ref, lng_ref, lnb_ref, wr_ref, br_ref,
               x1_ref, u2_ref, lg_ref):
    odn = odn_ref[...]
    z = z_ref[...]
    parts = []
    for h in range(DN_HEADS):
        cols = slice(h * DN_HEAD_DIM, (h + 1) * DN_HEAD_DIM)
        o = odn[:, cols]
        o = o * lax.rsqrt(jnp.mean(o * o, axis=-1, keepdims=True) + LN_EPS) * dnw_ref[...]
        parts.append((o * _silu(z[:, cols])).astype(BF16))
    y_dn = jnp.dot(jnp.concatenate(parts, axis=-1), wdn_ref[...], preferred_element_type=F32)
    y_da = jnp.dot(oda_ref[...].astype(BF16), wda_ref[...], preferred_element_type=F32)
    merged = _sigmoid(gdn_ref[...]) * y_dn + _sigmoid(gda_ref[...]) * y_da
    mix = jnp.dot(merged.astype(BF16), wo_ref[...], preferred_element_type=F32)
    r = alpha * x_ref[...] + gate_ref[0:1, :] * mix
    x1 = _layer_norm_rows(r, lng_ref[...], lnb_ref[...])
    x1_ref[...] = x1
    u2 = x1 * (1.0 + scale_ref[0:1, :]) + shift_ref[0:1, :]
    u2_ref[...] = u2
    lg_ref[...] = jnp.dot(u2, wr_ref[...], preferred_element_type=F32,
                          precision=HIGHEST) + br_ref[...]


def _post(alpha, x2, o_dn, proj, o_da, mods, dnw, wdn, wda, wo, lng, lnb, wr_pad, br_pad):
    s, d = x2.shape
    tm = min(s, 512)
    row = lambda i: (i, 0)
    const = lambda i: (0, 0)
    return pl.pallas_call(
        functools.partial(_post_body, alpha),
        out_shape=(jax.ShapeDtypeStruct((s, d), F32),
                   jax.ShapeDtypeStruct((s, d), F32),
                   jax.ShapeDtypeStruct((s, LANES), F32)),
        grid=(s // tm,),
        in_specs=[pl.BlockSpec((tm, d), row),
                  pl.BlockSpec((tm, DN_WIDTH), row),
                  pl.BlockSpec((tm, DN_WIDTH), lambda i: (i, COL_Z // DN_WIDTH)),
                  pl.BlockSpec((tm, DA_WIDTH), row),
                  pl.BlockSpec((tm, d), lambda i: (i, COL_GDN // d)),
                  pl.BlockSpec((tm, d), lambda i: (i, COL_GDA // d)),
                  pl.BlockSpec((SUBLANES, d), lambda i: (0, 2)),
                  pl.BlockSpec((SUBLANES, d), lambda i: (0, 4)),
                  pl.BlockSpec((SUBLANES, d), lambda i: (0, 3)),
                  pl.BlockSpec((1, DN_HEAD_DIM), const),
                  pl.BlockSpec((DN_WIDTH, d), const),
                  pl.BlockSpec((DA_WIDTH, d), const),
                  pl.BlockSpec((d, d), const),
                  pl.BlockSpec((1, d), const),
                  pl.BlockSpec((1, d), const),
                  pl.BlockSpec((d, LANES), const),
                  pl.BlockSpec((1, LANES), const)],
        out_specs=(pl.BlockSpec((tm, d), row), pl.BlockSpec((tm, d), row),
                   pl.BlockSpec((tm, LANES), row)),
        compiler_params=pltpu.CompilerParams(dimension_semantics=("arbitrary",),
                                             vmem_limit_bytes=VMEM_LIMIT),
        name="post",
    )(x2, o_dn, proj, o_da, proj, proj, mods, mods, mods, dnw, wdn, wda, wo, lng, lnb,
      wr_pad, br_pad)


ROUTE_T = 512


def _route_body(lg_ref, idx_ref, rank_ref, w_ref, cnt_ref, carry_ref):
    t = ROUTE_T

    @pl.when(pl.program_id(0) == 0)
    def _():
        carry_ref[...] = jnp.zeros_like(carry_ref)

    lane = lax.broadcasted_iota(I32, (t, LANES), 1)
    lane_f = lane.astype(F32)
    l = jnp.where(lane < N_EXPERTS, lg_ref[...], -jnp.inf)
    vals, idxs, sels = [], [], []
    for _ in range(TOP_K):
        mk = jnp.max(l, axis=-1, keepdims=True)
        ik = jnp.min(jnp.where(l == mk, lane_f, float(LANES)), axis=-1, keepdims=True)
        sel = lane_f == ik
        l = jnp.where(sel, -jnp.inf, l)
        vals.append(mk)
        idxs.append(ik)
        sels.append(sel)
    es = [jnp.exp(v - vals[0]) for v in vals]
    den = es[0] + es[1] + es[2] + es[3]
    onehot = jnp.zeros((t, LANES), F32)
    for sel in sels:
        onehot = jnp.where(sel, 1.0, onehot)
    ri = lax.broadcasted_iota(I32, (t, t), 0)
    ci = lax.broadcasted_iota(I32, (t, t), 1)
    before = (ci < ri).astype(BF16)
    rank_full = carry_ref[0:1, :] + jnp.dot(before, onehot.astype(BF16),
                                            preferred_element_type=F32)
    idx_out = jnp.zeros((t, LANES), F32)
    rank_out = jnp.zeros((t, LANES), F32)
    w_out = jnp.zeros((t, LANES), F32)
    for k in range(TOP_K):
        rk = jnp.sum(jnp.where(sels[k], rank_full, 0.0), axis=-1, keepdims=True)
        idx_out = jnp.where(lane == k, idxs[k], idx_out)
        rank_out = jnp.where(lane == k, rk, rank_out)
        w_out = jnp.where(lane == k, es[k] / den, w_out)
    idx_ref[...] = idx_out.astype(I32)
    rank_ref[...] = rank_out.astype(I32)
    w_ref[...] = w_out
    new = carry_ref[0:1, :] + jnp.sum(onehot, axis=0, keepdims=True)
    carry_ref[...] = jnp.broadcast_to(new, carry_ref.shape)
    cnt_ref[...] = jnp.broadcast_to(new, cnt_ref.shape).astype(I32)


def _route(logits):
    s = logits.shape[0]
    t = ROUTE_T
    row = pl.BlockSpec((t, LANES), lambda i: (i, 0))
    return pl.pallas_call(
        _route_body,
        out_shape=(jax.ShapeDtypeStruct((s, LANES), I32),
                   jax.ShapeDtypeStruct((s, LANES), I32),
                   jax.ShapeDtypeStruct((s, LANES), F32),
                   jax.ShapeDtypeStruct((SUBLANES, LANES), I32)),
        grid=(s // t,),
        in_specs=[row],
        out_specs=(row, row, row, pl.BlockSpec((SUBLANES, LANES), lambda i: (0, 0))),
        scratch_shapes=[pltpu.VMEM((SUBLANES, LANES), F32)],
        compiler_params=pltpu.CompilerParams(dimension_semantics=("arbitrary",),
                                             vmem_limit_bytes=VMEM_LIMIT),
        name="route",
    )(logits)


DISP_T = 256


def _dispatch_body(pstart_ref, cnt_ref, nused_ref, idx_ref, rank_ref, u_ref, xs_ref, zbuf_ref,
                   sem, zsem):
    t = DISP_T
    tm = MOE_TM
    nb = xs_ref.shape[0] // tm

    @pl.when(pl.program_id(0) == 0)
    def _():
        zbuf_ref[...] = jnp.zeros_like(zbuf_ref)

        def pad_copy(e, r):
            return pltpu.make_async_copy(zbuf_ref.at[pl.ds(0, 1), :],
                                         xs_ref.at[pl.ds(pstart_ref[e] + cnt_ref[e] + r, 1), :], zsem)

        def blk_copy(b):
            return pltpu.make_async_copy(zbuf_ref, xs_ref.at[pl.ds(pl.multiple_of(b * tm, tm), tm), :],
                                         zsem)

        def per_expert(fn):
            def body(e, _):
                n_pad = (-cnt_ref[e]) & (tm - 1)
                lax.fori_loop(0, n_pad, lambda r, _: (fn(pad_copy(e, r)), 0)[1], 0)
                return 0
            lax.fori_loop(0, N_EXPERTS, body, 0)

        per_expert(lambda cp: cp.start())
        lax.fori_loop(nused_ref[0], nb, lambda b, _: (blk_copy(b).start(), 0)[1], 0)
        per_expert(lambda cp: cp.wait())
        lax.fori_loop(nused_ref[0], nb, lambda b, _: (blk_copy(b).wait(), 0)[1], 0)

    def copy(r, k):
        n = r * TOP_K + k
        dest = pstart_ref[idx_ref[0, 0, n]] + rank_ref[0, 0, n]
        return pltpu.make_async_copy(u_ref.at[pl.ds(r, 1), :], xs_ref.at[pl.ds(dest, 1), :], sem)

    def start(r, _):
        for k in range(TOP_K):
            copy(r, k).start()
        return 0

    def wait(r, _):
        for k in range(TOP_K):
            copy(r, k).wait()
        return 0

    lax.fori_loop(0, t, start, 0)
    lax.fori_loop(0, t, wait, 0)


def _dispatch(pstart, counts, n_used, idx3, rank3, u2, n_rows):
    s, d = u2.shape
    t = DISP_T
    smem = lambda: pl.BlockSpec((1, 1, t * TOP_K), lambda i, *_: (i, 0, 0),
                                memory_space=pltpu.SMEM)
    return pl.pallas_call(
        _dispatch_body,
        out_shape=jax.ShapeDtypeStruct((n_rows, d), F32),
        grid_spec=pltpu.PrefetchScalarGridSpec(
            num_scalar_prefetch=3,
            grid=(s // t,),
            in_specs=[smem(), smem(), pl.BlockSpec((t, d), lambda i, *_: (i, 0))],
            out_specs=pl.BlockSpec(memory_space=pl.ANY),
            scratch_shapes=[pltpu.VMEM((MOE_TM, d), F32), pltpu.SemaphoreType.DMA,
                            pltpu.SemaphoreType.DMA]),
        compiler_params=pltpu.CompilerParams(dimension_semantics=("arbitrary",),
                                             vmem_limit_bytes=VMEM_LIMIT),
        name="dispatch",
    )(pstart, counts, n_used, idx3, rank3, u2)


PERM_W = 2 * LANES


def _experts_body(be_ref, nv_ref, nu_ref, x_ref, wgu_ref, bgu_ref, wd_ref, bd_ref, perm_ref,
                  o_ref, wgu_s, wd_s):
    b = pl.program_id(0)
    nvalid = nv_ref[b]
    prev = be_ref[jnp.maximum(b - 1, 0)]
    changed = jnp.logical_or(b == 0, be_ref[b] != prev)
    f = wgu_ref.shape[-1]

    @pl.when(jnp.logical_and(changed, nvalid > 0))
    def _():
        for g in range(f // PERM_W):
            cols = slice(g * PERM_W, (g + 1) * PERM_W)
            wgu_s[:, cols] = jnp.dot(wgu_ref[:, cols].astype(BF16), perm_ref[...],
                                     preferred_element_type=F32).astype(BF16)
        wd_s[...] = wd_ref[...].astype(BF16)

    @pl.when(nvalid == 0)
    def _():
        o_ref[...] = jnp.zeros_like(o_ref)

    @pl.when(nvalid > 0)
    def _():
        x = x_ref[...].astype(BF16)
        gu = jnp.dot(x, wgu_s[...], preferred_element_type=F32) + bgu_ref[...]
        acts = []
        for g in range(f // PERM_W):
            glu = jnp.minimum(gu[:, g * PERM_W:g * PERM_W + LANES], SWIGLU_LIMIT)
            lin = jnp.clip(gu[:, g * PERM_W + LANES:(g + 1) * PERM_W], -SWIGLU_LIMIT, SWIGLU_LIMIT)
            acts.append((glu * _sigmoid(SWIGLU_ALPHA * glu) * (lin + 1.0)).astype(BF16))
        act = jnp.concatenate(acts, axis=-1)
        o_ref[...] = jnp.dot(act, wd_s[...], preferred_element_type=F32) + bd_ref[...]


def _experts(block_expert, block_nvalid, n_used, xs, w_gate_up, bgu_perm, w_down, b_down3, perm):
    n_rows, d = xs.shape
    tm = MOE_TM
    nb = n_rows // tm
    e, _, f = w_gate_up.shape
    de = w_down.shape[1]
    blk = lambda b, be, nv, nu: (jnp.minimum(b, jnp.maximum(nu[0] - 1, 0)), 0)
    return pl.pallas_call(
        _experts_body,
        out_shape=jax.ShapeDtypeStruct((n_rows, d), F32),
        grid_spec=pltpu.PrefetchScalarGridSpec(
            num_scalar_prefetch=3,
            grid=(nb,),
            in_specs=[pl.BlockSpec((tm, d), blk),
                      pl.BlockSpec((None, d, f), lambda b, be, nv, nu: (be[b], 0, 0)),
                      pl.BlockSpec((None, 1, f), lambda b, be, nv, nu: (be[b], 0, 0)),
                      pl.BlockSpec((None, de, d), lambda b, be, nv, nu: (be[b], 0, 0)),
                      pl.BlockSpec((None, 1, d), lambda b, be, nv, nu: (be[b], 0, 0)),
                      pl.BlockSpec((PERM_W, PERM_W), lambda b, be, nv, nu: (0, 0))],
            out_specs=pl.BlockSpec((tm, d), lambda b, be, nv, nu: (b, 0)),
            scratch_shapes=[pltpu.VMEM((d, f), BF16), pltpu.VMEM((de, d), BF16)]),
        compiler_params=pltpu.CompilerParams(dimension_semantics=("arbitrary",),
                                             vmem_limit_bytes=VMEM_LIMIT),
        name="experts",
    )(block_expert, block_nvalid, n_used, xs, w_gate_up, bgu_perm, w_down, b_down3, perm)


COMB_T = 256


def _combine_body(alpha, pstart_ref, idx_ref, rank_ref, x1_ref, w_ref, gate_ref, lng_ref, lnb_ref,
                  ys_ref, o_ref, gbuf_ref, sem):
    t = COMB_T

    def copy(r, k):
        n = r * TOP_K + k
        src = pstart_ref[idx_ref[0, 0, n]] + rank_ref[0, 0, n]
        return pltpu.make_async_copy(ys_ref.at[pl.ds(src, 1), :], gbuf_ref.at[k, pl.ds(r, 1), :], sem)

    def start(r, _):
        for k in range(TOP_K):
            copy(r, k).start()
        return 0

    def wait(r, _):
        for k in range(TOP_K):
            copy(r, k).wait()
        return 0

    lax.fori_loop(0, t, start, 0)
    lax.fori_loop(0, t, wait, 0)
    w = w_ref[...]
    y = w[:, 0:1] * gbuf_ref[0]
    for k in range(1, TOP_K):
        y = y + w[:, k:k + 1] * gbuf_ref[k]
    r = alpha * x1_ref[...] + gate_ref[0:1, :] * y
    o_ref[...] = _layer_norm_rows(r, lng_ref[...], lnb_ref[...])


def _combine(alpha, pstart, idx3, rank3, x1, w_top, mods, lng, lnb, ys):
    s, d = x1.shape
    t = COMB_T
    smem = lambda: pl.BlockSpec((1, 1, t * TOP_K), lambda i, ps: (i, 0, 0),
                                memory_space=pltpu.SMEM)
    return pl.pallas_call(
        functools.partial(_combine_body, alpha),
        out_shape=jax.ShapeDtypeStruct((s, d), F32),
        grid_spec=pltpu.PrefetchScalarGridSpec(
            num_scalar_prefetch=1,
            grid=(s // t,),
            in_specs=[smem(), smem(),
                      pl.BlockSpec((t, d), lambda i, ps: (i, 0)),
                      pl.BlockSpec((t, LANES), lambda i, ps: (i, 0)),
                      pl.BlockSpec((SUBLANES, d), lambda i, ps: (0, 5)),
                      pl.BlockSpec((1, d), lambda i, ps: (0, 0)),
                      pl.BlockSpec((1, d), lambda i, ps: (0, 0)),
                      pl.BlockSpec(memory_space=pl.ANY)],
            out_specs=pl.BlockSpec((t, d), lambda i, ps: (i, 0)),
            scratch_shapes=[pltpu.VMEM((TOP_K, t, d), F32), pltpu.SemaphoreType.DMA]),
        compiler_params=pltpu.CompilerParams(dimension_semantics=("arbitrary",),
                                             vmem_limit_bytes=VMEM_LIMIT),
        name="combine",
    )(pstart, idx3, rank3, x1, w_top, mods, lng, lnb, ys)


def _perm_matrix():
    p = np.zeros((PERM_W, PERM_W), np.float32)
    for c in range(PERM_W):
        p[c, (c // 2) + (LANES if c % 2 else 0)] = 1.0
    return jnp.asarray(p, BF16)


def _pad_lanes(v, offset):
    return jnp.zeros((1, LANES), F32).at[0, offset:offset + v.shape[0]].set(v.astype(F32))


def _layer(x2, silu_in, positions, layer, w_ada, b_ada, w_in, conv_w, dn_a_log, dn_dt_bias,
           dn_norm_w, w_dn_proj, lambda_q1, lambda_k1, lambda_q2, lambda_k2, da_norm_w,
           w_da_proj, w_o, ln1_g, ln1_b, w_router, b_router, w_gate_up, b_gate_up, w_down,
           b_down, ln2_g, ln2_b, alpha):
    s, d = x2.shape
    mods = _mods(silu_in, w_ada, b_ada[None, :])

    offs = np.cumsum([0, DN_WIDTH, DN_WIDTH, DN_WIDTH, DN_WIDTH, DN_HEADS, DN_HEADS,
                      DA_WIDTH, DA_WIDTH, DA_WIDTH, d, d])
    seg = lambda n: w_in[:, offs[n]:offs[n + 1]]
    w_re = jnp.concatenate(
        [seg(9), seg(10), seg(0), seg(1), seg(2), seg(3), seg(6), seg(7), seg(8), seg(4), seg(5),
         jnp.zeros((d, NP_COLS - COL_BA - 2 * DN_HEADS), w_in.dtype)], axis=1).astype(BF16)
    proj = _inproj(x2, mods, w_re)

    half = DA_HEAD_DIM // 2
    inv_freq = ROPE_THETA ** (-jnp.arange(half, dtype=F32) / half)
    invf = jnp.tile(inv_freq, LANES // half)[None, :]
    sgn = jnp.tile(jnp.concatenate([-jnp.ones((half,), F32), jnp.ones((half,), F32)]),
                   LANES // DA_HEAD_DIM)[None, :]
    q1t, q2t, k12, vt3 = _attnprep(positions.reshape(s, 1), invf, sgn, proj)
    lam_init = 0.8 - 0.6 * math.exp(-0.3 * layer)
    o_da = _attn(lam_init, q1t, q2t, k12, vt3, lambda_q1[None, :], lambda_k1[None, :],
                 lambda_q2[None, :], lambda_k2[None, :], da_norm_w[:, None])

    o_dn = _deltanet(proj, conv_w, _pad_lanes(dn_a_log, DN_HEADS), _pad_lanes(dn_dt_bias, DN_HEADS))

    wr_pad = jnp.zeros((d, LANES), F32).at[:, :N_EXPERTS].set(w_router)
    x1, u2, logits = _post(alpha, x2, o_dn, proj, o_da, mods, dn_norm_w[None, :],
                           w_dn_proj.astype(BF16), w_da_proj.astype(BF16), w_o.astype(BF16),
                           ln1_g[None, :], ln1_b[None, :], wr_pad, _pad_lanes(b_router, 0))

    idx_l, rank_l, w_top, counts_l = _route(logits)
    counts = counts_l[0, :N_EXPERTS]
    tm = MOE_TM
    nblk_e = (counts + tm - 1) // tm
    blk_end = jnp.cumsum(nblk_e)
    blk_start = blk_end - nblk_e
    pstart = (blk_start * tm).astype(I32)
    nb = (s * TOP_K) // tm + N_EXPERTS
    n_used = blk_end[-1]
    bids = jnp.arange(nb, dtype=I32)
    last_e = jnp.max(jnp.where(counts > 0, jnp.arange(N_EXPERTS, dtype=I32), 0))
    be = jnp.sum((bids[:, None] >= blk_end[None, :]).astype(I32), axis=1)
    be = jnp.where(bids < n_used, jnp.minimum(be, N_EXPERTS - 1), last_e).astype(I32)
    nvalid = jnp.clip(counts[be] - (bids - blk_start[be]) * tm, 0, tm)
    nvalid = jnp.where(bids < n_used, nvalid, 0).astype(I32)
    idx3 = idx_l[:, :TOP_K].reshape(s // DISP_T, 1, DISP_T * TOP_K)
    rank3 = rank_l[:, :TOP_K].reshape(s // DISP_T, 1, DISP_T * TOP_K)

    n_used = n_used.astype(I32)[None]
    xs = _dispatch(pstart, counts.astype(I32), n_used, idx3, rank3, u2, nb * tm)
    e, _, f = w_gate_up.shape
    bgu_perm = b_gate_up.reshape(e, f // PERM_W, LANES, 2).transpose(0, 1, 3, 2).reshape(e, 1, f)
    ys = _experts(be, nvalid, n_used, xs, w_gate_up, bgu_perm, w_down,
                  b_down[:, None, :], _perm_matrix())
    return _combine(alpha, pstart, idx3, rank3, x1, w_top, mods, ln2_g[None, :], ln2_b[None, :], ys)


def kernel(x, c, positions, w_ada, b_ada, w_in, conv_w, dn_a_log, dn_dt_bias, dn_norm_w, w_dn_proj, lambda_q1, lambda_k1, lambda_q2, lambda_k2, da_norm_w, w_da_proj, w_o, ln1_g, ln1_b, w_router, b_router, w_gate_up, b_gate_up, w_down, b_down, ln2_g, ln2_b):
    b, s, d = x.shape
    assert b == 1, "kernel is written for a single sequence"
    depth = w_ada.shape[0]
    alpha = (2.0 * depth) ** 0.25
    c8 = jnp.zeros((SUBLANES, d), F32).at[0:1, :].set(c.astype(F32))
    x2 = x.reshape(s, d)
    for l in range(depth):
        x2 = _layer(x2, c8, positions, l, w_ada[l], b_ada[l], w_in[l], conv_w[l], dn_a_log[l],
                    dn_dt_bias[l], dn_norm_w[l], w_dn_proj[l], lambda_q1[l], lambda_k1[l],
                    lambda_q2[l], lambda_k2[l], da_norm_w[l], w_da_proj[l], w_o[l], ln1_g[l],
                    ln1_b[l], w_router[l], b_router[l], w_gate_up[l], b_gate_up[l], w_down[l],
                    b_down[l], ln2_g[l], ln2_b[l], alpha)
    return x2.reshape(b, s, d)
```

```python
import functools
import math

import numpy as np
import jax
import jax.numpy as jnp
from jax import lax
from jax.experimental import pallas as pl
from jax.experimental.pallas import tpu as pltpu

F32 = jnp.float32
BF16 = jnp.bfloat16
I32 = jnp.int32
HIGHEST = lax.Precision.HIGHEST

DN_HEADS = 4
DN_HEAD_DIM = 128
DN_WIDTH = DN_HEADS * DN_HEAD_DIM
CONV_WIDTH = 4
DN_CHUNK = 64
DA_HEADS = 4
DA_HEAD_DIM = 64
DA_VDIM = 2 * DA_HEAD_DIM
DA_WIDTH = DA_HEADS * DA_VDIM
ROPE_THETA = 10000.0
N_EXPERTS = 32
TOP_K = 4
SWIGLU_LIMIT = 7.0
SWIGLU_ALPHA = 1.702
LN_EPS = 1e-5
L2_EPS = 1e-6

LANES = 128
SUBLANES = 8
VMEM_LIMIT = 56 * 1024 * 1024

COL_GDN, COL_GDA, COL_DQ, COL_DK, COL_DV, COL_Z, COL_AQ, COL_AK, COL_AV, COL_BA = (
    0, 1024, 2048, 2560, 3072, 3584, 4096, 4608, 5120, 5632)
NP_COLS = 5760

MOE_TM = 256


def _sigmoid(x):
    return 1.0 / (1.0 + jnp.exp(-x))


def _silu(x):
    return x * _sigmoid(x)


def _bdot(a, b):
    return jnp.dot(a.astype(BF16), b.astype(BF16), preferred_element_type=F32)


def _bdot_nt(a, b):
    return lax.dot_general(a.astype(BF16), b.astype(BF16), (((1,), (1,)), ((), ())),
                           preferred_element_type=F32)


def _bdot_tn(a, b):
    return lax.dot_general(a.astype(BF16), b.astype(BF16), (((0,), (0,)), ((), ())),
                           preferred_element_type=F32)


def _layer_norm_rows(r, g, b):
    mu = jnp.mean(r, axis=-1, keepdims=True)
    d = r - mu
    var = jnp.mean(d * d, axis=-1, keepdims=True)
    return d * lax.rsqrt(var + LN_EPS) * g + b


def _mods_body(c_ref, w_ref, b_ref, o_ref):
    c = c_ref[...]
    o_ref[...] = jnp.dot(_silu(c), w_ref[...], preferred_element_type=F32,
                         precision=HIGHEST) + b_ref[...]


def _mods(c8, w_ada, b_ada):
    d, n = w_ada.shape
    tn = 1536
    return pl.pallas_call(
        _mods_body,
        out_shape=jax.ShapeDtypeStruct((SUBLANES, n), F32),
        grid=(n // tn,),
        in_specs=[pl.BlockSpec((SUBLANES, d), lambda j: (0, 0)),
                  pl.BlockSpec((d, tn), lambda j: (0, j)),
                  pl.BlockSpec((1, tn), lambda j: (0, j))],
        out_specs=pl.BlockSpec((SUBLANES, tn), lambda j: (0, j)),
        compiler_params=pltpu.CompilerParams(dimension_semantics=("arbitrary",),
                                             vmem_limit_bytes=VMEM_LIMIT),
        name="mods",
    )(c8, w_ada, b_ada)


def _inproj_body(x_ref, shift_ref, scale_ref, w_ref, o_ref, u_ref):
    @pl.when(pl.program_id(1) == 0)
    def _():
        u = x_ref[...] * (1.0 + scale_ref[0:1, :]) + shift_ref[0:1, :]
        u_ref[...] = u.astype(BF16)

    o_ref[...] = jnp.dot(u_ref[...], w_ref[...], preferred_element_type=F32)


def _inproj(x2, mods, w_re):
    s, d = x2.shape
    tm = min(s, 1024)
    tn = 1152
    return pl.pallas_call(
        _inproj_body,
        out_shape=jax.ShapeDtypeStruct((s, NP_COLS), F32),
        grid=(s // tm, NP_COLS // tn),
        in_specs=[pl.BlockSpec((tm, d), lambda i, j: (i, 0)),
                  pl.BlockSpec((SUBLANES, d), lambda i, j: (0, 0)),
                  pl.BlockSpec((SUBLANES, d), lambda i, j: (0, 1)),
                  pl.BlockSpec((d, tn), lambda i, j: (0, j))],
        out_specs=pl.BlockSpec((tm, tn), lambda i, j: (i, j)),
        scratch_shapes=[pltpu.VMEM((tm, d), BF16)],
        compiler_params=pltpu.CompilerParams(dimension_semantics=("arbitrary", "arbitrary"),
                                             vmem_limit_bytes=VMEM_LIMIT),
        name="inproj",
    )(x2, mods, mods, w_re)


ATT_T = 512


def _attnprep_body(pos_ref, invf_ref, sgn_ref, aq_ref, ak_ref, av_ref, qt_ref, k_ref, vt_ref):
    ang = pos_ref[...].astype(F32) * invf_ref[...]
    cos = jnp.cos(ang)
    sin = jnp.sin(ang) * sgn_ref[...]
    t = ang.shape[0]
    lane = lax.broadcasted_iota(I32, (t, LANES), 1)
    first_half = (lane & (DA_HEAD_DIM - 1)) < (DA_HEAD_DIM // 2)
    low_map = lane < DA_HEAD_DIM

    def rope(x):
        swapped = jnp.where(first_half, pltpu.roll(x, LANES - DA_HEAD_DIM // 2, 1),
                            pltpu.roll(x, DA_HEAD_DIM // 2, 1))
        return x * cos + swapped * sin

    scale = DA_HEAD_DIM ** -0.5 * math.log2(math.e)
    for h in range(DA_HEADS):
        cols = slice(h * LANES, (h + 1) * LANES)
        q = rope(aq_ref[:, cols]) * scale
        qt_ref[h, 0, :, 0:t] = jnp.where(low_map, q, 0.0).T.astype(BF16)
        qt_ref[h, 0, :, t:2 * t] = jnp.where(low_map, 0.0, q).T.astype(BF16)
        k_ref[:, cols] = rope(ak_ref[:, cols]).astype(BF16)
        vt_ref[h, 0] = av_ref[:, cols].T.astype(BF16)


def _attnprep(pos_col, invf, sgn, proj):
    s = proj.shape[0]
    t = ATT_T
    nt = s // t
    w = DA_WIDTH
    return pl.pallas_call(
        _attnprep_body,
        out_shape=(jax.ShapeDtypeStruct((DA_HEADS, nt, LANES, 2 * t), BF16),
                   jax.ShapeDtypeStruct((s, w), BF16),
                   jax.ShapeDtypeStruct((DA_HEADS, nt, LANES, t), BF16)),
        grid=(nt,),
        in_specs=[pl.BlockSpec((t, 1), lambda i: (i, 0)),
                  pl.BlockSpec((1, LANES), lambda i: (0, 0)),
                  pl.BlockSpec((1, LANES), lambda i: (0, 0)),
                  pl.BlockSpec((t, w), lambda i: (i, COL_AQ // w)),
                  pl.BlockSpec((t, w), lambda i: (i, COL_AK // w)),
                  pl.BlockSpec((t, w), lambda i: (i, COL_AV // w))],
        out_specs=(pl.BlockSpec((DA_HEADS, 1, LANES, 2 * t), lambda i: (0, i, 0, 0)),
                   pl.BlockSpec((t, w), lambda i: (i, 0)),
                   pl.BlockSpec((DA_HEADS, 1, LANES, t), lambda i: (0, i, 0, 0))),
        compiler_params=pltpu.CompilerParams(dimension_semantics=("arbitrary",),
                                             vmem_limit_bytes=VMEM_LIMIT),
        name="attnprep",
    )(pos_col, invf, sgn, proj, proj, proj)


def _attn_body(lam_init, qt_ref, k_ref, vt_ref, lq1_ref, lk1_ref, lq2_ref, lk2_ref,
               nw_ref, o_ref, acc_ref, s0_ref, s1_ref):
    t = ATT_T
    i = pl.program_id(1)
    qt = qt_ref[...]

    def scores(j, dst):
        start = pl.multiple_of(j * t, t)
        dst[...] = jnp.dot(k_ref[pl.ds(start, t), :], qt, preferred_element_type=F32)

    def update(j, src, m_old, l_old, masked):
        st = src[...]
        if masked:
            row = lax.broadcasted_iota(I32, (t, 2 * t), 0)
            col = lax.broadcasted_iota(I32, (t, 2 * t), 1) & (t - 1)
            st = jnp.where(row <= col, st, -jnp.inf)
        m_new = jnp.maximum(m_old, jnp.max(st, axis=0, keepdims=True))
        alpha = jnp.exp2(m_old - m_new)
        p = jnp.exp2(st - m_new)
        l_new = alpha * l_old + jnp.sum(p, axis=0, keepdims=True)
        acc_ref[...] = alpha * acc_ref[...] + jnp.dot(vt_ref[j], p.astype(BF16),
                                                      preferred_element_type=F32)
        return m_new, l_new

    def finalize(l):
        lam = (jnp.exp(jnp.sum(lq1_ref[...] * lk1_ref[...], axis=-1, keepdims=True))
               - jnp.exp(jnp.sum(lq2_ref[...] * lk2_ref[...], axis=-1, keepdims=True)) + lam_init)
        ot = acc_ref[:, 0:t] / l[:, 0:t] - lam * (acc_ref[:, t:2 * t] / l[:, t:2 * t])
        ms = jnp.mean(ot * ot, axis=0, keepdims=True)
        ot = ot * lax.rsqrt(ms + LN_EPS) * nw_ref[...] * (1.0 - lam_init)
        o_ref[...] = ot.T

    scores(0, s0_ref)
    acc_ref[...] = jnp.zeros_like(acc_ref)

    def pair(p, carry):
        j = 2 * p
        scores(j + 1, s1_ref)
        carry = update(j, s0_ref, *carry, False)
        scores(j + 2, s0_ref)
        return update(j + 1, s1_ref, *carry, False)

    m, l = lax.fori_loop(0, lax.shift_right_logical(i, 1), pair,
                         (jnp.full((1, 2 * t), -jnp.inf, F32), jnp.zeros((1, 2 * t), F32)))

    @pl.when((i & 1) == 0)
    def _():
        finalize(update(i, s0_ref, m, l, True)[1])

    @pl.when((i & 1) == 1)
    def _():
        scores(i, s1_ref)
        mid = update(i - 1, s0_ref, m, l, False)
        finalize(update(i, s1_ref, *mid, True)[1])


def _attn(lam_init, qt4, k12, vt3, lq1, lk1, lq2, lk2, nw_col):
    s = k12.shape[0]
    t = ATT_T
    nt = s // t
    small = pl.BlockSpec((1, DA_HEAD_DIM), lambda h, i: (0, 0))
    return pl.pallas_call(
        functools.partial(_attn_body, lam_init),
        out_shape=jax.ShapeDtypeStruct((s, DA_WIDTH), F32),
        grid=(DA_HEADS, nt),
        in_specs=[pl.BlockSpec((None, None, LANES, 2 * t), lambda h, i: (h, i, 0, 0)),
                  pl.BlockSpec((s, LANES), lambda h, i: (0, h)),
                  pl.BlockSpec((None, nt, LANES, t), lambda h, i: (h, 0, 0, 0)),
                  small, small, small, small,
                  pl.BlockSpec((DA_VDIM, 1), lambda h, i: (0, 0))],
        out_specs=pl.BlockSpec((t, LANES), lambda h, i: (i, h)),
        scratch_shapes=[pltpu.VMEM((LANES, 2 * t), F32),
                        pltpu.VMEM((t, 2 * t), F32), pltpu.VMEM((t, 2 * t), F32)],
        compiler_params=pltpu.CompilerParams(dimension_semantics=("arbitrary", "arbitrary"),
                                             vmem_limit_bytes=VMEM_LIMIT),
        name="attn",
    )(qt4, k12, vt3, lq1, lk1, lq2, lk2, nw_col)


DN_NC = 4
DN_TT = DN_NC * DN_CHUNK


def _tri_inverse(a_list, ii, jj, eye):
    zip_dot = lambda xs, ys: [_bdot(x, y) for x, y in zip(xs, ys)]
    blk8 = (ii >> 3) == (jj >> 3)
    n = [jnp.where(blk8, -a, 0.0) for a in a_list]
    n2 = zip_dot(n, n)
    n4 = zip_dot(n2, n2)
    nn2 = zip_dot(n, n2)
    p = [eye + x + y + z for x, y, z in zip(n, n2, nn2)]
    pn4 = zip_dot(p, n4)
    d = [x + y for x, y in zip(p, pn4)]
    for sh in (3, 4, 5):
        bi, bj = ii >> sh, jj >> sh
        below = ((bi & 1) == 1) & (bj == bi - 1)
        lo = [jnp.where(below, a, 0.0) for a in a_list]
        dld = zip_dot(zip_dot(d, lo), d)
        d = [x - y for x, y in zip(d, dld)]
    return d


def _cumsum_rows(x, row):
    shift = 1
    while shift < x.shape[0]:
        x = x + jnp.where(row >= shift, pltpu.roll(x, shift, 0), 0.0)
        shift *= 2
    return x


def _dn_body(q_ref, k_ref, v_ref, ba_ref, cw_ref, alog_ref, dtb_ref, o_ref, xbuf_ref, state_ref):
    tt = DN_TT
    c = DN_CHUNK
    nh = DN_HEADS
    dh = DN_HEAD_DIM
    r = nh * c

    @pl.when(pl.program_id(0) == 0)
    def _():
        xbuf_ref[:, 0:SUBLANES, :] = jnp.zeros((3, SUBLANES, DN_WIDTH), F32)
        state_ref[...] = jnp.zeros_like(state_ref)

    ys = []
    for n, ref in enumerate((q_ref, k_ref, v_ref)):
        cur = ref[...]
        xbuf_ref[n, SUBLANES:SUBLANES + tt, :] = cur
        acc = None
        for j in range(CONV_WIDTH):
            off = SUBLANES - (CONV_WIDTH - 1) + j
            term = cw_ref[j:j + 1, n * DN_WIDTH:(n + 1) * DN_WIDTH] * xbuf_ref[n, off:off + tt, :]
            acc = term if acc is None else acc + term
        xbuf_ref[n, 0:SUBLANES, :] = cur[tt - SUBLANES:tt, :]
        ys.append(_silu(acc))
    yq, yk, yv = ys

    ba = ba_ref[...]
    beta_full = _sigmoid(ba)
    g_full = -jnp.exp(alog_ref[...]) * jax.nn.softplus(ba + dtb_ref[...])

    qn, kn, vv = [], [], []
    for h in range(nh):
        cols = slice(h * dh, (h + 1) * dh)
        qh, kh = yq[:, cols], yk[:, cols]
        qn.append(qh * lax.rsqrt(jnp.sum(qh * qh, axis=-1, keepdims=True) + L2_EPS) * (dh ** -0.5))
        kn.append(kh * lax.rsqrt(jnp.sum(kh * kh, axis=-1, keepdims=True) + L2_EPS))
        vv.append(yv[:, cols])

    ii = lax.broadcasted_iota(I32, (r, r), 0)
    jj = lax.broadcasted_iota(I32, (r, r), 1)
    eye = (ii == jj).astype(F32)
    same_head = (ii >> 6) == (jj >> 6)
    incl = same_head & (ii >= jj)
    strict = same_head & (ii > jj)
    row_c = lax.broadcasted_iota(I32, (c, LANES), 0)
    chunks = range(DN_NC)

    def head_cols(x, lane0):
        return jnp.concatenate([x[:, lane0 + h:lane0 + h + 1] for h in range(nh)], axis=0)

    def head_rows(x, lane0):
        xt = jnp.concatenate([x, jnp.zeros_like(x)], axis=0).T
        return jnp.concatenate([xt[lane0 + h:lane0 + h + 1, 0:c] for h in range(nh)], axis=1)

    def stack(parts, rows):
        return jnp.concatenate([p[rows, :] for p in parts], axis=0)

    rows_of = [slice(ci * c, (ci + 1) * c) for ci in chunks]
    cg = [_cumsum_rows(g_full[rows, :], row_c) for rows in rows_of]
    tot = [x[c - 1:c, :] for x in cg]
    eg_st = [head_cols(jnp.exp(x), nh) for x in cg]
    erev_st = [head_cols(jnp.exp(t - x), nh) for x, t in zip(cg, tot)]
    etot = [jnp.exp(t) for t in tot]
    beta_st = [head_cols(beta_full[rows, :], 0) for rows in rows_of]
    decay = [jnp.exp(jnp.where(incl, head_cols(x, nh) - head_rows(x, nh), -jnp.inf)) for x in cg]
    k_st = [stack(kn, rows) for rows in rows_of]
    q_st = [stack(qn, rows) for rows in rows_of]
    v_st = [stack(vv, rows) for rows in rows_of]
    kk = [_bdot_nt(k, k) for k in k_st]
    qk = [_bdot_nt(q, k) * dc for q, k, dc in zip(q_st, k_st, decay)]
    a = [jnp.where(strict, b * dc * x, 0.0) for b, dc, x in zip(beta_st, decay, kk)]
    tinv = _tri_inverse(a, ii, jj, eye)
    rhs = [jnp.concatenate([b * v, (b * e) * k], axis=1)
           for b, e, k, v in zip(beta_st, eg_st, k_st, v_st)]
    uw = [_bdot(t, x) for t, x in zip(tinv, rhs)]
    qw = [_bdot(x, y) for x, y in zip(qk, uw)]
    o_loc = [x[:, 0:dh] for x in qw]
    q_eff = [q * e - x[:, dh:2 * dh] for q, e, x in zip(q_st, eg_st, qw)]
    k_dec = [k * e for k, e in zip(k_st, erev_st)]
    pb = [[_bdot_tn(kd[h * c:(h + 1) * c, :], x[h * c:(h + 1) * c, :]) for h in range(nh)]
          for kd, x in zip(k_dec, uw)]

    states = [state_ref[h] for h in range(nh)]
    for ci in chunks:
        for h in range(nh):
            hrows = slice(h * c, (h + 1) * c)
            lhs = jnp.concatenate([pb[ci][h][:, dh:2 * dh], q_eff[ci][hrows, :]], axis=0)
            ps = _bdot(lhs, states[h])
            gl = nh + h
            states[h] = states[h] * etot[ci][:, gl:gl + 1] - ps[0:dh, :] + pb[ci][h][:, 0:dh]
            o_ref[rows_of[ci], h * dh:(h + 1) * dh] = ps[dh:dh + c, :] + o_loc[ci][hrows, :]
    for h in range(nh):
        state_ref[h] = states[h]


def _deltanet(proj, conv_w, alog_row, dtb_row):
    s = proj.shape[0]
    tt = DN_TT
    w = DN_WIDTH
    return pl.pallas_call(
        _dn_body,
        out_shape=jax.ShapeDtypeStruct((s, w), F32),
        grid=(s // tt,),
        in_specs=[pl.BlockSpec((tt, w), lambda i: (i, COL_DQ // w)),
                  pl.BlockSpec((tt, w), lambda i: (i, COL_DK // w)),
                  pl.BlockSpec((tt, w), lambda i: (i, COL_DV // w)),
                  pl.BlockSpec((tt, LANES), lambda i: (i, COL_BA // LANES)),
                  pl.BlockSpec((CONV_WIDTH, 3 * w), lambda i: (0, 0)),
                  pl.BlockSpec((1, LANES), lambda i: (0, 0)),
                  pl.BlockSpec((1, LANES), lambda i: (0, 0))],
        out_specs=pl.BlockSpec((tt, w), lambda i: (i, 0)),
        scratch_shapes=[pltpu.VMEM((3, tt + SUBLANES, w), F32),
                        pltpu.VMEM((DN_HEADS, DN_HEAD_DIM, DN_HEAD_DIM), F32)],
        compiler_params=pltpu.CompilerParams(dimension_semantics=("arbitrary",),
                                             vmem_limit_bytes=VMEM_LIMIT),
        name="deltanet",
    )(proj, proj, proj, proj, conv_w, alog_row, dtb_row)


def _post_body(alpha, x_ref, odn_ref, z_ref, oda_ref, gdn_ref, gda_ref, gate_ref, scale_ref,
               shift_ref, dnw_ref, wdn_ref, wda_ref, wo_ref, lng_ref, lnb_ref, wr_ref, br_ref,
               x1_ref, u2_ref, lg_ref):
    odn = odn_ref[...]
    z = z_ref[...]
    parts = []
    for h in range(DN_HEADS):
        cols = slice(h * DN_HEAD_DIM, (h + 1) * DN_HEAD_DIM)
        o = odn[:, cols]
        o = o * lax.rsqrt(jnp.mean(o * o, axis=-1, keepdims=True) + LN_EPS) * dnw_ref[...]
        parts.append((o * _silu(z[:, cols])).astype(BF16))
    y_dn = jnp.dot(jnp.concatenate(parts, axis=-1), wdn_ref[...], preferred_element_type=F32)
    y_da = jnp.dot(oda_ref[...].astype(BF16), wda_ref[...], preferred_element_type=F32)
    merged = _sigmoid(gdn_ref[...]) * y_dn + _sigmoid(gda_ref[...]) * y_da
    mix = jnp.dot(merged.astype(BF16), wo_ref[...], preferred_element_type=F32)
    r = alpha * x_ref[...] + gate_ref[0:1, :] * mix
    x1 = _layer_norm_rows(r, lng_ref[...], lnb_ref[...])
    x1_ref[...] = x1
    u2 = x1 * (1.0 + scale_ref[0:1, :]) + shift_ref[0:1, :]
    u2_ref[...] = u2
    lg_ref[...] = jnp.dot(u2, wr_ref[...], preferred_element_type=F32,
                          precision=HIGHEST) + br_ref[...]


def _post(alpha, x2, o_dn, proj, o_da, mods, dnw, wdn, wda, wo, lng, lnb, wr_pad, br_pad):
    s, d = x2.shape
    tm = min(s, 512)
    row = lambda i: (i, 0)
    const = lambda i: (0, 0)
    return pl.pallas_call(
        functools.partial(_post_body, alpha),
        out_shape=(jax.ShapeDtypeStruct((s, d), F32),
                   jax.ShapeDtypeStruct((s, d), F32),
                   jax.ShapeDtypeStruct((s, LANES), F32)),
        grid=(s // tm,),
        in_specs=[pl.BlockSpec((tm, d), row),
                  pl.BlockSpec((tm, DN_WIDTH), row),
                  pl.BlockSpec((tm, DN_WIDTH), lambda i: (i, COL_Z // DN_WIDTH)),
                  pl.BlockSpec((tm, DA_WIDTH), row),
                  pl.BlockSpec((tm, d), lambda i: (i, COL_GDN // d)),
                  pl.BlockSpec((tm, d), lambda i: (i, COL_GDA // d)),
                  pl.BlockSpec((SUBLANES, d), lambda i: (0, 2)),
                  pl.BlockSpec((SUBLANES, d), lambda i: (0, 4)),
                  pl.BlockSpec((SUBLANES, d), lambda i: (0, 3)),
                  pl.BlockSpec((1, DN_HEAD_DIM), const),
                  pl.BlockSpec((DN_WIDTH, d), const),
                  pl.BlockSpec((DA_WIDTH, d), const),
                  pl.BlockSpec((d, d), const),
                  pl.BlockSpec((1, d), const),
                  pl.BlockSpec((1, d), const),
                  pl.BlockSpec((d, LANES), const),
                  pl.BlockSpec((1, LANES), const)],
        out_specs=(pl.BlockSpec((tm, d), row), pl.BlockSpec((tm, d), row),
                   pl.BlockSpec((tm, LANES), row)),
        compiler_params=pltpu.CompilerParams(dimension_semantics=("arbitrary",),
                                             vmem_limit_bytes=VMEM_LIMIT),
        name="post",
    )(x2, o_dn, proj, o_da, proj, proj, mods, mods, mods, dnw, wdn, wda, wo, lng, lnb,
      wr_pad, br_pad)


ROUTE_T = 512


def _route_body(lg_ref, idx_ref, rank_ref, w_ref, cnt_ref, carry_ref):
    t = ROUTE_T

    @pl.when(pl.program_id(0) == 0)
    def _():
        carry_ref[...] = jnp.zeros_like(carry_ref)

    lane = lax.broadcasted_iota(I32, (t, LANES), 1)
    lane_f = lane.astype(F32)
    l = jnp.where(lane < N_EXPERTS, lg_ref[...], -jnp.inf)
    vals, idxs, sels = [], [], []
    for _ in range(TOP_K):
        mk = jnp.max(l, axis=-1, keepdims=True)
        ik = jnp.min(jnp.where(l == mk, lane_f, float(LANES)), axis=-1, keepdims=True)
        sel = lane_f == ik
        l = jnp.where(sel, -jnp.inf, l)
        vals.append(mk)
        idxs.append(ik)
        sels.append(sel)
    es = [jnp.exp(v - vals[0]) for v in vals]
    den = es[0] + es[1] + es[2] + es[3]
    onehot = jnp.zeros((t, LANES), F32)
    for sel in sels:
        onehot = jnp.where(sel, 1.0, onehot)
    ri = lax.broadcasted_iota(I32, (t, t), 0)
    ci = lax.broadcasted_iota(I32, (t, t), 1)
    before = (ci < ri).astype(BF16)
    rank_full = carry_ref[0:1, :] + jnp.dot(before, onehot.astype(BF16),
                                            preferred_element_type=F32)
    idx_out = jnp.zeros((t, LANES), F32)
    rank_out = jnp.zeros((t, LANES), F32)
    w_out = jnp.zeros((t, LANES), F32)
    for k in range(TOP_K):
        rk = jnp.sum(jnp.where(sels[k], rank_full, 0.0), axis=-1, keepdims=True)
        idx_out = jnp.where(lane == k, idxs[k], idx_out)
        rank_out = jnp.where(lane == k, rk, rank_out)
        w_out = jnp.where(lane == k, es[k] / den, w_out)
    idx_ref[...] = idx_out.astype(I32)
    rank_ref[...] = rank_out.astype(I32)
    w_ref[...] = w_out
    new = carry_ref[0:1, :] + jnp.sum(onehot, axis=0, keepdims=True)
    carry_ref[...] = jnp.broadcast_to(new, carry_ref.shape)
    cnt_ref[...] = jnp.broadcast_to(new, cnt_ref.shape).astype(I32)


def _route(logits):
    s = logits.shape[0]
    t = ROUTE_T
    row = pl.BlockSpec((t, LANES), lambda i: (i, 0))
    return pl.pallas_call(
        _route_body,
        out_shape=(jax.ShapeDtypeStruct((s, LANES), I32),
                   jax.ShapeDtypeStruct((s, LANES), I32),
                   jax.ShapeDtypeStruct((s, LANES), F32),
                   jax.ShapeDtypeStruct((SUBLANES, LANES), I32)),
        grid=(s // t,),
        in_specs=[row],
        out_specs=(row, row, row, pl.BlockSpec((SUBLANES, LANES), lambda i: (0, 0))),
        scratch_shapes=[pltpu.VMEM((SUBLANES, LANES), F32)],
        compiler_params=pltpu.CompilerParams(dimension_semantics=("arbitrary",),
                                             vmem_limit_bytes=VMEM_LIMIT),
        name="route",
    )(logits)


DISP_T = 256


def _dispatch_body(pstart_ref, cnt_ref, nused_ref, idx_ref, rank_ref, u_ref, xs_ref, zbuf_ref,
                   sem, zsem):
    t = DISP_T
    tm = MOE_TM
    nb = xs_ref.shape[0] // tm

    @pl.when(pl.program_id(0) == 0)
    def _():
        zbuf_ref[...] = jnp.zeros_like(zbuf_ref)

        def pad_copy(e, r):
            return pltpu.make_async_copy(zbuf_ref.at[pl.ds(0, 1), :],
                                         xs_ref.at[pl.ds(pstart_ref[e] + cnt_ref[e] + r, 1), :], zsem)

        def blk_copy(b):
            return pltpu.make_async_copy(zbuf_ref, xs_ref.at[pl.ds(pl.multiple_of(b * tm, tm), tm), :],
                                         zsem)

        def per_expert(fn):
            def body(e, _):
                n_pad = (-cnt_ref[e]) & (tm - 1)
                lax.fori_loop(0, n_pad, lambda r, _: (fn(pad_copy(e, r)), 0)[1], 0)
                return 0
            lax.fori_loop(0, N_EXPERTS, body, 0)

        per_expert(lambda cp: cp.start())
        lax.fori_loop(nused_ref[0], nb, lambda b, _: (blk_copy(b).start(), 0)[1], 0)
        per_expert(lambda cp: cp.wait())
        lax.fori_loop(nused_ref[0], nb, lambda b, _: (blk_copy(b).wait(), 0)[1], 0)

    def copy(r, k):
        n = r * TOP_K + k
        dest = pstart_ref[idx_ref[0, 0, n]] + rank_ref[0, 0, n]
        return pltpu.make_async_copy(u_ref.at[pl.ds(r, 1), :], xs_ref.at[pl.ds(dest, 1), :], sem)

    def start(r, _):
        for k in range(TOP_K):
            copy(r, k).start()
        return 0

    def wait(r, _):
        for k in range(TOP_K):
            copy(r, k).wait()
        return 0

    lax.fori_loop(0, t, start, 0)
    lax.fori_loop(0, t, wait, 0)


def _dispatch(pstart, counts, n_used, idx3, rank3, u2, n_rows):
    s, d = u2.shape
    t = DISP_T
    smem = lambda: pl.BlockSpec((1, 1, t * TOP_K), lambda i, *_: (i, 0, 0),
                                memory_space=pltpu.SMEM)
    return pl.pallas_call(
        _dispatch_body,
        out_shape=jax.ShapeDtypeStruct((n_rows, d), F32),
        grid_spec=pltpu.PrefetchScalarGridSpec(
            num_scalar_prefetch=3,
            grid=(s // t,),
            in_specs=[smem(), smem(), pl.BlockSpec((t, d), lambda i, *_: (i, 0))],
            out_specs=pl.BlockSpec(memory_space=pl.ANY),
            scratch_shapes=[pltpu.VMEM((MOE_TM, d), F32), pltpu.SemaphoreType.DMA,
                            pltpu.SemaphoreType.DMA]),
        compiler_params=pltpu.CompilerParams(dimension_semantics=("arbitrary",),
                                             vmem_limit_bytes=VMEM_LIMIT),
        name="dispatch",
    )(pstart, counts, n_used, idx3, rank3, u2)


PERM_W = 2 * LANES


def _experts_body(be_ref, nv_ref, nu_ref, x_ref, wgu_ref, bgu_ref, wd_ref, bd_ref, perm_ref,
                  o_ref, wgu_s, wd_s):
    b = pl.program_id(0)
    nvalid = nv_ref[b]
    prev = be_ref[jnp.maximum(b - 1, 0)]
    changed = jnp.logical_or(b == 0, be_ref[b] != prev)
    f = wgu_ref.shape[-1]

    @pl.when(jnp.logical_and(changed, nvalid > 0))
    def _():
        for g in range(f // PERM_W):
            cols = slice(g * PERM_W, (g + 1) * PERM_W)
            wgu_s[:, cols] = jnp.dot(wgu_ref[:, cols].astype(BF16), perm_ref[...],
                                     preferred_element_type=F32).astype(BF16)
        wd_s[...] = wd_ref[...].astype(BF16)

    @pl.when(nvalid == 0)
    def _():
        o_ref[...] = jnp.zeros_like(o_ref)

    @pl.when(nvalid > 0)
    def _():
        x = x_ref[...].astype(BF16)
        gu = jnp.dot(x, wgu_s[...], preferred_element_type=F32) + bgu_ref[...]
        acts = []
        for g in range(f // PERM_W):
            glu = jnp.minimum(gu[:, g * PERM_W:g * PERM_W + LANES], SWIGLU_LIMIT)
            lin = jnp.clip(gu[:, g * PERM_W + LANES:(g + 1) * PERM_W], -SWIGLU_LIMIT, SWIGLU_LIMIT)
            acts.append((glu * _sigmoid(SWIGLU_ALPHA * glu) * (lin + 1.0)).astype(BF16))
        act = jnp.concatenate(acts, axis=-1)
        o_ref[...] = jnp.dot(act, wd_s[...], preferred_element_type=F32) + bd_ref[...]


def _experts(block_expert, block_nvalid, n_used, xs, w_gate_up, bgu_perm, w_down, b_down3, perm):
    n_rows, d = xs.shape
    tm = MOE_TM
    nb = n_rows // tm
    e, _, f = w_gate_up.shape
    de = w_down.shape[1]
    blk = lambda b, be, nv, nu: (jnp.minimum(b, jnp.maximum(nu[0] - 1, 0)), 0)
    return pl.pallas_call(
        _experts_body,
        out_shape=jax.ShapeDtypeStruct((n_rows, d), F32),
        grid_spec=pltpu.PrefetchScalarGridSpec(
            num_scalar_prefetch=3,
            grid=(nb,),
            in_specs=[pl.BlockSpec((tm, d), blk),
                      pl.BlockSpec((None, d, f), lambda b, be, nv, nu: (be[b], 0, 0)),
                      pl.BlockSpec((None, 1, f), lambda b, be, nv, nu: (be[b], 0, 0)),
                      pl.BlockSpec((None, de, d), lambda b, be, nv, nu: (be[b], 0, 0)),
                      pl.BlockSpec((None, 1, d), lambda b, be, nv, nu: (be[b], 0, 0)),
                      pl.BlockSpec((PERM_W, PERM_W), lambda b, be, nv, nu: (0, 0))],
            out_specs=pl.BlockSpec((tm, d), lambda b, be, nv, nu: (b, 0)),
            scratch_shapes=[pltpu.VMEM((d, f), BF16), pltpu.VMEM((de, d), BF16)]),
        compiler_params=pltpu.CompilerParams(dimension_semantics=("arbitrary",),
                                             vmem_limit_bytes=VMEM_LIMIT),
        name="experts",
    )(block_expert, block_nvalid, n_used, xs, w_gate_up, bgu_perm, w_down, b_down3, perm)


COMB_T = 256


def _combine_body(alpha, pstart_ref, idx_ref, rank_ref, x1_ref, w_ref, gate_ref, lng_ref, lnb_ref,
                  ys_ref, o_ref, gbuf_ref, sem):
    t = COMB_T

    def copy(r, k):
        n = r * TOP_K + k
        src = pstart_ref[idx_ref[0, 0, n]] + rank_ref[0, 0, n]
        return pltpu.make_async_copy(ys_ref.at[pl.ds(src, 1), :], gbuf_ref.at[k, pl.ds(r, 1), :], sem)

    def start(r, _):
        for k in range(TOP_K):
            copy(r, k).start()
        return 0

    def wait(r, _):
        for k in range(TOP_K):
            copy(r, k).wait()
        return 0

    lax.fori_loop(0, t, start, 0)
    lax.fori_loop(0, t, wait, 0)
    w = w_ref[...]
    y = w[:, 0:1] * gbuf_ref[0]
    for k in range(1, TOP_K):
        y = y + w[:, k:k + 1] * gbuf_ref[k]
    r = alpha * x1_ref[...] + gate_ref[0:1, :] * y
    o_ref[...] = _layer_norm_rows(r, lng_ref[...], lnb_ref[...])


def _combine(alpha, pstart, idx3, rank3, x1, w_top, mods, lng, lnb, ys):
    s, d = x1.shape
    t = COMB_T
    smem = lambda: pl.BlockSpec((1, 1, t * TOP_K), lambda i, ps: (i, 0, 0),
                                memory_space=pltpu.SMEM)
    return pl.pallas_call(
        functools.partial(_combine_body, alpha),
        out_shape=jax.ShapeDtypeStruct((s, d), F32),
        grid_spec=pltpu.PrefetchScalarGridSpec(
            num_scalar_prefetch=1,
            grid=(s // t,),
            in_specs=[smem(), smem(),
                      pl.BlockSpec((t, d), lambda i, ps: (i, 0)),
                      pl.BlockSpec((t, LANES), lambda i, ps: (i, 0)),
                      pl.BlockSpec((SUBLANES, d), lambda i, ps: (0, 5)),
                      pl.BlockSpec((1, d), lambda i, ps: (0, 0)),
                      pl.BlockSpec((1, d), lambda i, ps: (0, 0)),
                      pl.BlockSpec(memory_space=pl.ANY)],
            out_specs=pl.BlockSpec((t, d), lambda i, ps: (i, 0)),
            scratch_shapes=[pltpu.VMEM((TOP_K, t, d), F32), pltpu.SemaphoreType.DMA]),
        compiler_params=pltpu.CompilerParams(dimension_semantics=("arbitrary",),
                                             vmem_limit_bytes=VMEM_LIMIT),
        name="combine",
    )(pstart, idx3, rank3, x1, w_top, mods, lng, lnb, ys)


def _perm_matrix():
    p = np.zeros((PERM_W, PERM_W), np.float32)
    for c in range(PERM_W):
        p[c, (c // 2) + (LANES if c % 2 else 0)] = 1.0
    return jnp.asarray(p, BF16)


def _pad_lanes(v, offset):
    return jnp.zeros((1, LANES), F32).at[0, offset:offset + v.shape[0]].set(v.astype(F32))


def _layer(x2, silu_in, positions, layer, w_ada, b_ada, w_in, conv_w, dn_a_log, dn_dt_bias,
           dn_norm_w, w_dn_proj, lambda_q1, lambda_k1, lambda_q2, lambda_k2, da_norm_w,
           w_da_proj, w_o, ln1_g, ln1_b, w_router, b_router, w_gate_up, b_gate_up, w_down,
           b_down, ln2_g, ln2_b, alpha):
    s, d = x2.shape
    mods = _mods(silu_in, w_ada, b_ada[None, :])

    offs = np.cumsum([0, DN_WIDTH, DN_WIDTH, DN_WIDTH, DN_WIDTH, DN_HEADS, DN_HEADS,
                      DA_WIDTH, DA_WIDTH, DA_WIDTH, d, d])
    seg = lambda n: w_in[:, offs[n]:offs[n + 1]]
    w_re = jnp.concatenate(
        [seg(9), seg(10), seg(0), seg(1), seg(2), seg(3), seg(6), seg(7), seg(8), seg(4), seg(5),
         jnp.zeros((d, NP_COLS - COL_BA - 2 * DN_HEADS), w_in.dtype)], axis=1).astype(BF16)
    proj = _inproj(x2, mods, w_re)

    half = DA_HEAD_DIM // 2
    inv_freq = ROPE_THETA ** (-jnp.arange(half, dtype=F32) / half)
    invf = jnp.tile(inv_freq, LANES // half)[None, :]
    sgn = jnp.tile(jnp.concatenate([-jnp.ones((half,), F32), jnp.ones((half,), F32)]),
                   LANES // DA_HEAD_DIM)[None, :]
    qt4, k12, vt3 = _attnprep(positions.reshape(s, 1), invf, sgn, proj)
    lam_init = 0.8 - 0.6 * math.exp(-0.3 * layer)
    o_da = _attn(lam_init, qt4, k12, vt3, lambda_q1[None, :], lambda_k1[None, :],
                 lambda_q2[None, :], lambda_k2[None, :], da_norm_w[:, None])

    o_dn = _deltanet(proj, conv_w, _pad_lanes(dn_a_log, DN_HEADS), _pad_lanes(dn_dt_bias, DN_HEADS))

    wr_pad = jnp.zeros((d, LANES), F32).at[:, :N_EXPERTS].set(w_router)
    x1, u2, logits = _post(alpha, x2, o_dn, proj, o_da, mods, dn_norm_w[None, :],
                           w_dn_proj.astype(BF16), w_da_proj.astype(BF16), w_o.astype(BF16),
                           ln1_g[None, :], ln1_b[None, :], wr_pad, _pad_lanes(b_router, 0))

    idx_l, rank_l, w_top, counts_l = _route(logits)
    counts = counts_l[0, :N_EXPERTS]
    tm = MOE_TM
    nblk_e = (counts + tm - 1) // tm
    blk_end = jnp.cumsum(nblk_e)
    blk_start = blk_end - nblk_e
    pstart = (blk_start * tm).astype(I32)
    nb = (s * TOP_K) // tm + N_EXPERTS
    n_used = blk_end[-1]
    bids = jnp.arange(nb, dtype=I32)
    last_e = jnp.max(jnp.where(counts > 0, jnp.arange(N_EXPERTS, dtype=I32), 0))
    be = jnp.sum((bids[:, None] >= blk_end[None, :]).astype(I32), axis=1)
    be = jnp.where(bids < n_used, jnp.minimum(be, N_EXPERTS - 1), last_e).astype(I32)
    nvalid = jnp.clip(counts[be] - (bids - blk_start[be]) * tm, 0, tm)
    nvalid = jnp.where(bids < n_used, nvalid, 0).astype(I32)
    idx3 = idx_l[:, :TOP_K].reshape(s // DISP_T, 1, DISP_T * TOP_K)
    rank3 = rank_l[:, :TOP_K].reshape(s // DISP_T, 1, DISP_T * TOP_K)

    n_used = n_used.astype(I32)[None]
    xs = _dispatch(pstart, counts.astype(I32), n_used, idx3, rank3, u2, nb * tm)
    e, _, f = w_gate_up.shape
    bgu_perm = b_gate_up.reshape(e, f // PERM_W, LANES, 2).transpose(0, 1, 3, 2).reshape(e, 1, f)
    ys = _experts(be, nvalid, n_used, xs, w_gate_up, bgu_perm, w_down,
                  b_down[:, None, :], _perm_matrix())
    return _combine(alpha, pstart, idx3, rank3, x1, w_top, mods, ln2_g[None, :], ln2_b[None, :], ys)


def kernel(x, c, positions, w_ada, b_ada, w_in, conv_w, dn_a_log, dn_dt_bias, dn_norm_w, w_dn_proj, lambda_q1, lambda_k1, lambda_q2, lambda_k2, da_norm_w, w_da_proj, w_o, ln1_g, ln1_b, w_router, b_router, w_gate_up, b_gate_up, w_down, b_down, ln2_g, ln2_b):
    b, s, d = x.shape
    assert b == 1, "kernel is written for a single sequence"
    depth = w_ada.shape[0]
    alpha = (2.0 * depth) ** 0.25
    c8 = jnp.zeros((SUBLANES, d), F32).at[0:1, :].set(c.astype(F32))
    x2 = x.reshape(s, d)
    for l in range(depth):
        x2 = _layer(x2, c8, positions, l, w_ada[l], b_ada[l], w_in[l], conv_w[l], dn_a_log[l],
                    dn_dt_bias[l], dn_norm_w[l], w_dn_proj[l], lambda_q1[l], lambda_k1[l],
                    lambda_q2[l], lambda_k2[l], da_norm_w[l], w_da_proj[l], w_o[l], ln1_g[l],
                    ln1_b[l], w_router[l], b_router[l], w_gate_up[l], b_gate_up[l], w_down[l],
                    b_down[l], ln2_g[l], ln2_b[l], alpha)
    return x2.reshape(b, s, d)
```

```python
import functools
import math

import numpy as np
import jax
import jax.numpy as jnp
from jax import lax
from jax.experimental import pallas as pl
from jax.experimental.pallas import tpu as pltpu

F32 = jnp.float32
BF16 = jnp.bfloat16
I32 = jnp.int32
HIGHEST = lax.Precision.HIGHEST

DN_HEADS = 4
DN_HEAD_DIM = 128
DN_WIDTH = DN_HEADS * DN_HEAD_DIM
CONV_WIDTH = 4
DN_CHUNK = 64
DA_HEADS = 4
DA_HEAD_DIM = 64
DA_VDIM = 2 * DA_HEAD_DIM
DA_WIDTH = DA_HEADS * DA_VDIM
ROPE_THETA = 10000.0
N_EXPERTS = 32
TOP_K = 4
SWIGLU_LIMIT = 7.0
SWIGLU_ALPHA = 1.702
LN_EPS = 1e-5
L2_EPS = 1e-6

LANES = 128
SUBLANES = 8
VMEM_LIMIT = 56 * 1024 * 1024

COL_GDN, COL_GDA, COL_DQ, COL_DK, COL_DV, COL_Z, COL_AQ, COL_AK, COL_AV, COL_BA = (
    0, 1024, 2048, 2560, 3072, 3584, 4096, 4608, 5120, 5632)
NP_COLS = 5760

MOE_TM = 256


def _sigmoid(x):
    return 1.0 / (1.0 + jnp.exp(-x))


def _silu(x):
    return x * _sigmoid(x)


def _bdot(a, b):
    return jnp.dot(a.astype(BF16), b.astype(BF16), preferred_element_type=F32)


def _bdot_nt(a, b):
    return lax.dot_general(a.astype(BF16), b.astype(BF16), (((1,), (1,)), ((), ())),
                           preferred_element_type=F32)


def _bdot_tn(a, b):
    return lax.dot_general(a.astype(BF16), b.astype(BF16), (((0,), (0,)), ((), ())),
                           preferred_element_type=F32)


def _layer_norm_rows(r, g, b):
    mu = jnp.mean(r, axis=-1, keepdims=True)
    d = r - mu
    var = jnp.mean(d * d, axis=-1, keepdims=True)
    return d * lax.rsqrt(var + LN_EPS) * g + b


def _mods_body(c_ref, w_ref, b_ref, o_ref):
    c = c_ref[...]
    o_ref[...] = jnp.dot(_silu(c), w_ref[...], preferred_element_type=F32,
                         precision=HIGHEST) + b_ref[...]


def _mods(c8, w_ada, b_ada):
    d, n = w_ada.shape
    tn = 1536
    return pl.pallas_call(
        _mods_body,
        out_shape=jax.ShapeDtypeStruct((SUBLANES, n), F32),
        grid=(n // tn,),
        in_specs=[pl.BlockSpec((SUBLANES, d), lambda j: (0, 0)),
                  pl.BlockSpec((d, tn), lambda j: (0, j)),
                  pl.BlockSpec((1, tn), lambda j: (0, j))],
        out_specs=pl.BlockSpec((SUBLANES, tn), lambda j: (0, j)),
        compiler_params=pltpu.CompilerParams(dimension_semantics=("arbitrary",),
                                             vmem_limit_bytes=VMEM_LIMIT),
        name="mods",
    )(c8, w_ada, b_ada)


def _inproj_body(x_ref, shift_ref, scale_ref, w_ref, o_ref, u_ref):
    @pl.when(pl.program_id(1) == 0)
    def _():
        u = x_ref[...] * (1.0 + scale_ref[0:1, :]) + shift_ref[0:1, :]
        u_ref[...] = u.astype(BF16)

    o_ref[...] = jnp.dot(u_ref[...], w_ref[...], preferred_element_type=F32)


def _inproj(x2, mods, w_re):
    s, d = x2.shape
    tm = min(s, 1024)
    tn = 1152
    return pl.pallas_call(
        _inproj_body,
        out_shape=jax.ShapeDtypeStruct((s, NP_COLS), F32),
        grid=(s // tm, NP_COLS // tn),
        in_specs=[pl.BlockSpec((tm, d), lambda i, j: (i, 0)),
                  pl.BlockSpec((SUBLANES, d), lambda i, j: (0, 0)),
                  pl.BlockSpec((SUBLANES, d), lambda i, j: (0, 1)),
                  pl.BlockSpec((d, tn), lambda i, j: (0, j))],
        out_specs=pl.BlockSpec((tm, tn), lambda i, j: (i, j)),
        scratch_shapes=[pltpu.VMEM((tm, d), BF16)],
        compiler_params=pltpu.CompilerParams(dimension_semantics=("arbitrary", "arbitrary"),
                                             vmem_limit_bytes=VMEM_LIMIT),
        name="inproj",
    )(x2, mods, mods, w_re)


ATT_T = 512


def _attnprep_body(pos_ref, invf_ref, sgn_ref, aq_ref, ak_ref, av_ref, qt_ref, k_ref, vt_ref):
    ang = pos_ref[...].astype(F32) * invf_ref[...]
    cos = jnp.cos(ang)
    sin = jnp.sin(ang) * sgn_ref[...]
    t = ang.shape[0]
    lane = lax.broadcasted_iota(I32, (t, LANES), 1)
    first_half = (lane & (DA_HEAD_DIM - 1)) < (DA_HEAD_DIM // 2)
    low_map = lane < DA_HEAD_DIM

    def rope(x):
        swapped = jnp.where(first_half, pltpu.roll(x, LANES - DA_HEAD_DIM // 2, 1),
                            pltpu.roll(x, DA_HEAD_DIM // 2, 1))
        return x * cos + swapped * sin

    scale = DA_HEAD_DIM ** -0.5 * math.log2(math.e)
    for h in range(DA_HEADS):
        cols = slice(h * LANES, (h + 1) * LANES)
        q = rope(aq_ref[:, cols]) * scale
        qt_ref[h, 0, :, 0:t] = jnp.where(low_map, q, 0.0).T.astype(BF16)
        qt_ref[h, 0, :, t:2 * t] = jnp.where(low_map, 0.0, q).T.astype(BF16)
        k_ref[:, cols] = rope(ak_ref[:, cols]).astype(BF16)
        vt_ref[h, 0] = av_ref[:, cols].T.astype(BF16)


def _attnprep(pos_col, invf, sgn, proj):
    s = proj.shape[0]
    t = ATT_T
    nt = s // t
    w = DA_WIDTH
    return pl.pallas_call(
        _attnprep_body,
        out_shape=(jax.ShapeDtypeStruct((DA_HEADS, nt, LANES, 2 * t), BF16),
                   jax.ShapeDtypeStruct((s, w), BF16),
                   jax.ShapeDtypeStruct((DA_HEADS, nt, LANES, t), BF16)),
        grid=(nt,),
        in_specs=[pl.BlockSpec((t, 1), lambda i: (i, 0)),
                  pl.BlockSpec((1, LANES), lambda i: (0, 0)),
                  pl.BlockSpec((1, LANES), lambda i: (0, 0)),
                  pl.BlockSpec((t, w), lambda i: (i, COL_AQ // w)),
                  pl.BlockSpec((t, w), lambda i: (i, COL_AK // w)),
                  pl.BlockSpec((t, w), lambda i: (i, COL_AV // w))],
        out_specs=(pl.BlockSpec((DA_HEADS, 1, LANES, 2 * t), lambda i: (0, i, 0, 0)),
                   pl.BlockSpec((t, w), lambda i: (i, 0)),
                   pl.BlockSpec((DA_HEADS, 1, LANES, t), lambda i: (0, i, 0, 0))),
        compiler_params=pltpu.CompilerParams(dimension_semantics=("arbitrary",),
                                             vmem_limit_bytes=VMEM_LIMIT),
        name="attnprep",
    )(pos_col, invf, sgn, proj, proj, proj)


def _attn_body(lam_init, qt_ref, k_ref, vt_ref, lq1_ref, lk1_ref, lq2_ref, lk2_ref,
               nw_ref, o_ref, acc_ref, s0_ref, s1_ref):
    t = ATT_T
    i = pl.program_id(1)
    qt = qt_ref[...]

    def scores(j, dst):
        start = pl.multiple_of(j * t, t)
        dst[...] = jnp.dot(k_ref[pl.ds(start, t), :], qt, preferred_element_type=F32)

    def update(j, src, m_old, l_old, masked):
        st = src[...]
        if masked:
            row = lax.broadcasted_iota(I32, (t, 2 * t), 0)
            col = lax.broadcasted_iota(I32, (t, 2 * t), 1) & (t - 1)
            st = jnp.where(row <= col, st, -jnp.inf)
        m_new = jnp.maximum(m_old, jnp.max(st, axis=0, keepdims=True))
        alpha = jnp.exp2(m_old - m_new)
        p = jnp.exp2(st - m_new)
        l_new = alpha * l_old + jnp.sum(p, axis=0, keepdims=True)
        acc_ref[...] = alpha * acc_ref[...] + jnp.dot(vt_ref[j], p.astype(BF16),
                                                      preferred_element_type=F32)
        return m_new, l_new

    def finalize(l):
        lam = (jnp.exp(jnp.sum(lq1_ref[...] * lk1_ref[...], axis=-1, keepdims=True))
               - jnp.exp(jnp.sum(lq2_ref[...] * lk2_ref[...], axis=-1, keepdims=True)) + lam_init)
        ot = acc_ref[:, 0:t] / l[:, 0:t] - lam * (acc_ref[:, t:2 * t] / l[:, t:2 * t])
        ms = jnp.mean(ot * ot, axis=0, keepdims=True)
        ot = ot * lax.rsqrt(ms + LN_EPS) * nw_ref[...] * (1.0 - lam_init)
        o_ref[...] = ot.T

    scores(0, s0_ref)
    acc_ref[...] = jnp.zeros_like(acc_ref)

    def pair(j, carry):
        scores(j + 1, s1_ref)
        carry = update(j, s0_ref, *carry, False)
        scores(j + 2, s0_ref)
        return update(j + 1, s1_ref, *carry, False)

    def quad(p, carry):
        return pair(4 * p + 2, pair(4 * p, carry))

    n_quad = lax.shift_right_logical(i, 2)
    carry = lax.fori_loop(0, n_quad, quad,
                          (jnp.full((1, 2 * t), -jnp.inf, F32), jnp.zeros((1, 2 * t), F32)))
    n_pair = lax.shift_right_logical(i & 3, 1)
    m, l = lax.fori_loop(0, n_pair, lambda p, c: pair(4 * n_quad + 2 * p, c), carry)

    @pl.when((i & 1) == 0)
    def _():
        finalize(update(i, s0_ref, m, l, True)[1])

    @pl.when((i & 1) == 1)
    def _():
        scores(i, s1_ref)
        mid = update(i - 1, s0_ref, m, l, False)
        finalize(update(i, s1_ref, *mid, True)[1])


def _attn(lam_init, qt4, k12, vt3, lq1, lk1, lq2, lk2, nw_col):
    s = k12.shape[0]
    t = ATT_T
    nt = s // t
    small = pl.BlockSpec((1, DA_HEAD_DIM), lambda h, i: (0, 0))
    return pl.pallas_call(
        functools.partial(_attn_body, lam_init),
        out_shape=jax.ShapeDtypeStruct((s, DA_WIDTH), F32),
        grid=(DA_HEADS, nt),
        in_specs=[pl.BlockSpec((None, None, LANES, 2 * t), lambda h, i: (h, i, 0, 0)),
                  pl.BlockSpec((s, LANES), lambda h, i: (0, h)),
                  pl.BlockSpec((None, nt, LANES, t), lambda h, i: (h, 0, 0, 0)),
                  small, small, small, small,
                  pl.BlockSpec((DA_VDIM, 1), lambda h, i: (0, 0))],
        out_specs=pl.BlockSpec((t, LANES), lambda h, i: (i, h)),
        scratch_shapes=[pltpu.VMEM((LANES, 2 * t), F32),
                        pltpu.VMEM((t, 2 * t), F32), pltpu.VMEM((t, 2 * t), F32)],
        compiler_params=pltpu.CompilerParams(dimension_semantics=("arbitrary", "arbitrary"),
                                             vmem_limit_bytes=VMEM_LIMIT),
        name="attn",
    )(qt4, k12, vt3, lq1, lk1, lq2, lk2, nw_col)


DN_NC = 4
DN_TT = DN_NC * DN_CHUNK


def _tri_inverse(a_list, ii, jj, eye):
    zip_dot = lambda xs, ys: [_bdot(x, y) for x, y in zip(xs, ys)]
    blk8 = (ii >> 3) == (jj >> 3)
    n = [jnp.where(blk8, -a, 0.0) for a in a_list]
    n2 = zip_dot(n, n)
    n4 = zip_dot(n2, n2)
    nn2 = zip_dot(n, n2)
    p = [eye + x + y + z for x, y, z in zip(n, n2, nn2)]
    pn4 = zip_dot(p, n4)
    d = [x + y for x, y in zip(p, pn4)]
    for sh in (3, 4, 5):
        bi, bj = ii >> sh, jj >> sh
        below = ((bi & 1) == 1) & (bj == bi - 1)
        lo = [jnp.where(below, a, 0.0) for a in a_list]
        dld = zip_dot(zip_dot(d, lo), d)
        d = [x - y for x, y in zip(d, dld)]
    return d


def _cumsum_rows(x, row):
    shift = 1
    while shift < x.shape[0]:
        x = x + jnp.where(row >= shift, pltpu.roll(x, shift, 0), 0.0)
        shift *= 2
    return x


def _dn_body(q_ref, k_ref, v_ref, ba_ref, cw_ref, alog_ref, dtb_ref, o_ref, xbuf_ref, state_ref):
    tt = DN_TT
    c = DN_CHUNK
    nh = DN_HEADS
    dh = DN_HEAD_DIM
    r = nh * c

    @pl.when(pl.program_id(0) == 0)
    def _():
        xbuf_ref[:, 0:SUBLANES, :] = jnp.zeros((3, SUBLANES, DN_WIDTH), F32)
        state_ref[...] = jnp.zeros_like(state_ref)

    ys = []
    for n, ref in enumerate((q_ref, k_ref, v_ref)):
        cur = ref[...]
        xbuf_ref[n, SUBLANES:SUBLANES + tt, :] = cur
        acc = None
        for j in range(CONV_WIDTH):
            off = SUBLANES - (CONV_WIDTH - 1) + j
            term = cw_ref[j:j + 1, n * DN_WIDTH:(n + 1) * DN_WIDTH] * xbuf_ref[n, off:off + tt, :]
            acc = term if acc is None else acc + term
        xbuf_ref[n, 0:SUBLANES, :] = cur[tt - SUBLANES:tt, :]
        ys.append(_silu(acc))
    yq, yk, yv = ys

    ba = ba_ref[...]
    beta_full = _sigmoid(ba)
    g_full = -jnp.exp(alog_ref[...]) * jax.nn.softplus(ba + dtb_ref[...])

    qn, kn, vv = [], [], []
    for h in range(nh):
        cols = slice(h * dh, (h + 1) * dh)
        qh, kh = yq[:, cols], yk[:, cols]
        qn.append(qh * lax.rsqrt(jnp.sum(qh * qh, axis=-1, keepdims=True) + L2_EPS) * (dh ** -0.5))
        kn.append(kh * lax.rsqrt(jnp.sum(kh * kh, axis=-1, keepdims=True) + L2_EPS))
        vv.append(yv[:, cols])

    ii = lax.broadcasted_iota(I32, (r, r), 0)
    jj = lax.broadcasted_iota(I32, (r, r), 1)
    eye = (ii == jj).astype(F32)
    same_head = (ii >> 6) == (jj >> 6)
    incl = same_head & (ii >= jj)
    strict = same_head & (ii > jj)
    row_c = lax.broadcasted_iota(I32, (c, LANES), 0)
    chunks = range(DN_NC)

    def head_cols(x, lane0):
        return jnp.concatenate([x[:, lane0 + h:lane0 + h + 1] for h in range(nh)], axis=0)

    def head_rows(x, lane0):
        xt = jnp.concatenate([x, jnp.zeros_like(x)], axis=0).T
        return jnp.concatenate([xt[lane0 + h:lane0 + h + 1, 0:c] for h in range(nh)], axis=1)

    def stack(parts, rows):
        return jnp.concatenate([p[rows, :] for p in parts], axis=0)

    rows_of = [slice(ci * c, (ci + 1) * c) for ci in chunks]
    cg = [_cumsum_rows(g_full[rows, :], row_c) for rows in rows_of]
    tot = [x[c - 1:c, :] for x in cg]
    eg_st = [head_cols(jnp.exp(x), nh) for x in cg]
    erev_st = [head_cols(jnp.exp(t - x), nh) for x, t in zip(cg, tot)]
    etot = [jnp.exp(t) for t in tot]
    beta_st = [head_cols(beta_full[rows, :], 0) for rows in rows_of]
    decay = [jnp.exp(jnp.where(incl, head_cols(x, nh) - head_rows(x, nh), -jnp.inf)) for x in cg]
    k_st = [stack(kn, rows) for rows in rows_of]
    q_st = [stack(qn, rows) for rows in rows_of]
    v_st = [stack(vv, rows) for rows in rows_of]
    kk = [_bdot_nt(k, k) for k in k_st]
    qk = [_bdot_nt(q, k) * dc for q, k, dc in zip(q_st, k_st, decay)]
    a = [jnp.where(strict, b * dc * x, 0.0) for b, dc, x in zip(beta_st, decay, kk)]
    tinv = _tri_inverse(a, ii, jj, eye)
    rhs = [jnp.concatenate([b * v, (b * e) * k], axis=1)
           for b, e, k, v in zip(beta_st, eg_st, k_st, v_st)]
    uw = [_bdot(t, x) for t, x in zip(tinv, rhs)]
    qw = [_bdot(x, y) for x, y in zip(qk, uw)]
    o_loc = [x[:, 0:dh] for x in qw]
    q_eff = [q * e - x[:, dh:2 * dh] for q, e, x in zip(q_st, eg_st, qw)]
    k_dec = [k * e for k, e in zip(k_st, erev_st)]
    pb = [[_bdot_tn(kd[h * c:(h + 1) * c, :], x[h * c:(h + 1) * c, :]) for h in range(nh)]
          for kd, x in zip(k_dec, uw)]

    states = [state_ref[h] for h in range(nh)]
    for ci in chunks:
        for h in range(nh):
            hrows = slice(h * c, (h + 1) * c)
            lhs = jnp.concatenate([pb[ci][h][:, dh:2 * dh], q_eff[ci][hrows, :]], axis=0)
            ps = _bdot(lhs, states[h])
            gl = nh + h
            states[h] = states[h] * etot[ci][:, gl:gl + 1] - ps[0:dh, :] + pb[ci][h][:, 0:dh]
            o_ref[rows_of[ci], h * dh:(h + 1) * dh] = ps[dh:dh + c, :] + o_loc[ci][hrows, :]
    for h in range(nh):
        state_ref[h] = states[h]


def _deltanet(proj, conv_w, alog_row, dtb_row):
    s = proj.shape[0]
    tt = DN_TT
    w = DN_WIDTH
    return pl.pallas_call(
        _dn_body,
        out_shape=jax.ShapeDtypeStruct((s, w), F32),
        grid=(s // tt,),
        in_specs=[pl.BlockSpec((tt, w), lambda i: (i, COL_DQ // w)),
                  pl.BlockSpec((tt, w), lambda i: (i, COL_DK // w)),
                  pl.BlockSpec((tt, w), lambda i: (i, COL_DV // w)),
                  pl.BlockSpec((tt, LANES), lambda i: (i, COL_BA // LANES)),
                  pl.BlockSpec((CONV_WIDTH, 3 * w), lambda i: (0, 0)),
                  pl.BlockSpec((1, LANES), lambda i: (0, 0)),
                  pl.BlockSpec((1, LANES), lambda i: (0, 0))],
        out_specs=pl.BlockSpec((tt, w), lambda i: (i, 0)),
        scratch_shapes=[pltpu.VMEM((3, tt + SUBLANES, w), F32),
                        pltpu.VMEM((DN_HEADS, DN_HEAD_DIM, DN_HEAD_DIM), F32)],
        compiler_params=pltpu.CompilerParams(dimension_semantics=("arbitrary",),
                                             vmem_limit_bytes=VMEM_LIMIT),
        name="deltanet",
    )(proj, proj, proj, proj, conv_w, alog_row, dtb_row)


POST_SUB = 128

def _post_body(alpha, x_ref, odn_ref, z_ref, oda_ref, gdn_ref, gda_ref, gate_ref, scale_ref,
               shift_ref, dnw_ref, wdn_ref, wda_ref, wo_ref, lng_ref, lnb_ref, wr_ref, br_ref,
               x1_ref, u2_ref, lg_ref):
    for r0 in range(0, x_ref.shape[0], POST_SUB):
        rows = slice(r0, r0 + POST_SUB)
        odn = odn_ref[rows, :]
        z = z_ref[rows, :]
        parts = []
        for h in range(DN_HEADS):
            cols = slice(h * DN_HEAD_DIM, (h + 1) * DN_HEAD_DIM)
            o = odn[:, cols]
            o = o * lax.rsqrt(jnp.mean(o * o, axis=-1, keepdims=True) + LN_EPS) * dnw_ref[...]
            parts.append((o * _silu(z[:, cols])).astype(BF16))
        y_dn = jnp.dot(jnp.concatenate(parts, axis=-1), wdn_ref[...], preferred_element_type=F32)
        y_da = jnp.dot(oda_ref[rows, :].astype(BF16), wda_ref[...], preferred_element_type=F32)
        merged = _sigmoid(gdn_ref[rows, :]) * y_dn + _sigmoid(gda_ref[rows, :]) * y_da
        mix = jnp.dot(merged.astype(BF16), wo_ref[...], preferred_element_type=F32)
        r = alpha * x_ref[rows, :] + gate_ref[0:1, :] * mix
        x1 = _layer_norm_rows(r, lng_ref[...], lnb_ref[...])
        x1_ref[rows, :] = x1
        u2 = x1 * (1.0 + scale_ref[0:1, :]) + shift_ref[0:1, :]
        u2_ref[rows, :] = u2
        lg_ref[rows, :] = jnp.dot(u2, wr_ref[...], preferred_element_type=F32,
                                  precision=HIGHEST) + br_ref[...]


def _post(alpha, x2, o_dn, proj, o_da, mods, dnw, wdn, wda, wo, lng, lnb, wr_pad, br_pad):
    s, d = x2.shape
    tm = min(s, 512)
    row = lambda i: (i, 0)
    const = lambda i: (0, 0)
    return pl.pallas_call(
        functools.partial(_post_body, alpha),
        out_shape=(jax.ShapeDtypeStruct((s, d), F32),
                   jax.ShapeDtypeStruct((s, d), F32),
                   jax.ShapeDtypeStruct((s, LANES), F32)),
        grid=(s // tm,),
        in_specs=[pl.BlockSpec((tm, d), row),
                  pl.BlockSpec((tm, DN_WIDTH), row),
                  pl.BlockSpec((tm, DN_WIDTH), lambda i: (i, COL_Z // DN_WIDTH)),
                  pl.BlockSpec((tm, DA_WIDTH), row),
                  pl.BlockSpec((tm, d), lambda i: (i, COL_GDN // d)),
                  pl.BlockSpec((tm, d), lambda i: (i, COL_GDA // d)),
                  pl.BlockSpec((SUBLANES, d), lambda i: (0, 2)),
                  pl.BlockSpec((SUBLANES, d), lambda i: (0, 4)),
                  pl.BlockSpec((SUBLANES, d), lambda i: (0, 3)),
                  pl.BlockSpec((1, DN_HEAD_DIM), const),
                  pl.BlockSpec((DN_WIDTH, d), const),
                  pl.BlockSpec((DA_WIDTH, d), const),
                  pl.BlockSpec((d, d), const),
                  pl.BlockSpec((1, d), const),
                  pl.BlockSpec((1, d), const),
                  pl.BlockSpec((d, LANES), const),
                  pl.BlockSpec((1, LANES), const)],
        out_specs=(pl.BlockSpec((tm, d), row), pl.BlockSpec((tm, d), row),
                   pl.BlockSpec((tm, LANES), row)),
        compiler_params=pltpu.CompilerParams(dimension_semantics=("arbitrary",),
                                             vmem_limit_bytes=VMEM_LIMIT),
        name="post",
    )(x2, o_dn, proj, o_da, proj, proj, mods, mods, mods, dnw, wdn, wda, wo, lng, lnb,
      wr_pad, br_pad)


ROUTE_T = 512


def _route_body(lg_ref, idx_ref, rank_ref, w_ref, cnt_ref, carry_ref):
    t = ROUTE_T

    @pl.when(pl.program_id(0) == 0)
    def _():
        carry_ref[...] = jnp.zeros_like(carry_ref)

    lane = lax.broadcasted_iota(I32, (t, LANES), 1)
    lane_f = lane.astype(F32)
    l = jnp.where(lane < N_EXPERTS, lg_ref[...], -jnp.inf)
    vals, idxs, sels = [], [], []
    for _ in range(TOP_K):
        mk = jnp.max(l, axis=-1, keepdims=True)
        ik = jnp.min(jnp.where(l == mk, lane_f, float(LANES)), axis=-1, keepdims=True)
        sel = lane_f == ik
        l = jnp.where(sel, -jnp.inf, l)
        vals.append(mk)
        idxs.append(ik)
        sels.append(sel)
    es = [jnp.exp(v - vals[0]) for v in vals]
    den = es[0] + es[1] + es[2] + es[3]
    onehot = jnp.zeros((t, LANES), F32)
    for sel in sels:
        onehot = jnp.where(sel, 1.0, onehot)
    ri = lax.broadcasted_iota(I32, (t, t), 0)
    ci = lax.broadcasted_iota(I32, (t, t), 1)
    before = (ci < ri).astype(BF16)
    rank_full = carry_ref[0:1, :] + jnp.dot(before, onehot.astype(BF16),
                                            preferred_element_type=F32)
    idx_out = jnp.zeros((t, LANES), F32)
    rank_out = jnp.zeros((t, LANES), F32)
    w_out = jnp.zeros((t, LANES), F32)
    for k in range(TOP_K):
        rk = jnp.sum(jnp.where(sels[k], rank_full, 0.0), axis=-1, keepdims=True)
        idx_out = jnp.where(lane == k, idxs[k], idx_out)
        rank_out = jnp.where(lane == k, rk, rank_out)
        w_out = jnp.where(lane == k, es[k] / den, w_out)
    idx_ref[...] = idx_out.astype(I32)
    rank_ref[...] = rank_out.astype(I32)
    w_ref[...] = w_out
    new = carry_ref[0:1, :] + jnp.sum(onehot, axis=0, keepdims=True)
    carry_ref[...] = jnp.broadcast_to(new, carry_ref.shape)
    cnt_ref[...] = jnp.broadcast_to(new, cnt_ref.shape).astype(I32)


def _route(logits):
    s = logits.shape[0]
    t = ROUTE_T
    row = pl.BlockSpec((t, LANES), lambda i: (i, 0))
    return pl.pallas_call(
        _route_body,
        out_shape=(jax.ShapeDtypeStruct((s, LANES), I32),
                   jax.ShapeDtypeStruct((s, LANES), I32),
                   jax.ShapeDtypeStruct((s, LANES), F32),
                   jax.ShapeDtypeStruct((SUBLANES, LANES), I32)),
        grid=(s // t,),
        in_specs=[row],
        out_specs=(row, row, row, pl.BlockSpec((SUBLANES, LANES), lambda i: (0, 0))),
        scratch_shapes=[pltpu.VMEM((SUBLANES, LANES), F32)],
        compiler_params=pltpu.CompilerParams(dimension_semantics=("arbitrary",),
                                             vmem_limit_bytes=VMEM_LIMIT),
        name="route",
    )(logits)


DISP_T = 512


def _dispatch_body(pstart_ref, cnt_ref, nused_ref, dest_ref, u_ref, xs_ref, zbuf_ref, sem, zsem):
    t = DISP_T
    tm = MOE_TM
    nb = xs_ref.shape[0] // tm

    @pl.when(pl.program_id(0) == 0)
    def _():
        zbuf_ref[...] = jnp.zeros_like(zbuf_ref)

        def pad_copy(e, r):
            return pltpu.make_async_copy(zbuf_ref.at[pl.ds(0, 1), :],
                                         xs_ref.at[pl.ds(pstart_ref[e] + cnt_ref[e] + r, 1), :], zsem)

        def blk_copy(b):
            return pltpu.make_async_copy(zbuf_ref, xs_ref.at[pl.ds(pl.multiple_of(b * tm, tm), tm), :],
                                         zsem)

        def per_expert(fn):
            def body(e, _):
                n_pad = (-cnt_ref[e]) & (tm - 1)
                lax.fori_loop(0, n_pad, lambda r, _: (fn(pad_copy(e, r)), 0)[1], 0)
                return 0
            lax.fori_loop(0, N_EXPERTS, body, 0)

        per_expert(lambda cp: cp.start())
        lax.fori_loop(nused_ref[0], nb, lambda b, _: (blk_copy(b).start(), 0)[1], 0)
        per_expert(lambda cp: cp.wait())
        lax.fori_loop(nused_ref[0], nb, lambda b, _: (blk_copy(b).wait(), 0)[1], 0)

    def copy(r, k):
        dest = dest_ref[0, 0, r * TOP_K + k]
        return pltpu.make_async_copy(u_ref.at[pl.ds(r, 1), :], xs_ref.at[pl.ds(dest, 1), :], sem)

    def start(r, _):
        for k in range(TOP_K):
            copy(r, k).start(priority=k % 2)
        return 0

    def wait(r, _):
        for k in range(TOP_K):
            copy(r, k).wait()
        return 0

    lax.fori_loop(0, t, start, 0, unroll=2)
    lax.fori_loop(0, t, wait, 0, unroll=8)


def _dispatch(pstart, counts, n_used, dest3, u2, n_rows):
    s, d = u2.shape
    t = DISP_T
    smem = lambda: pl.BlockSpec((1, 1, t * TOP_K), lambda i, *_: (i, 0, 0),
                                memory_space=pltpu.SMEM)
    return pl.pallas_call(
        _dispatch_body,
        out_shape=jax.ShapeDtypeStruct((n_rows, d), F32),
        grid_spec=pltpu.PrefetchScalarGridSpec(
            num_scalar_prefetch=3,
            grid=(s // t,),
            in_specs=[smem(), pl.BlockSpec((t, d), lambda i, *_: (i, 0))],
            out_specs=pl.BlockSpec(memory_space=pl.ANY),
            scratch_shapes=[pltpu.VMEM((MOE_TM, d), F32), pltpu.SemaphoreType.DMA,
                            pltpu.SemaphoreType.DMA]),
        compiler_params=pltpu.CompilerParams(dimension_semantics=("arbitrary",),
                                             vmem_limit_bytes=VMEM_LIMIT),
        name="dispatch",
    )(pstart, counts, n_used, dest3, u2)


PERM_W = 2 * LANES


def _experts_body(be_ref, nv_ref, nu_ref, x_ref, wgu_ref, bgu_ref, wd_ref, bd_ref, perm_ref,
                  o_ref, wgu_s, wd_s):
    b = pl.program_id(0)
    nvalid = nv_ref[b]
    prev = be_ref[jnp.maximum(b - 1, 0)]
    changed = jnp.logical_or(b == 0, be_ref[b] != prev)
    f = wgu_ref.shape[-1]

    @pl.when(jnp.logical_and(changed, nvalid > 0))
    def _():
        for g in range(f // PERM_W):
            cols = slice(g * PERM_W, (g + 1) * PERM_W)
            wgu_s[:, cols] = jnp.dot(wgu_ref[:, cols].astype(BF16), perm_ref[...],
                                     preferred_element_type=F32).astype(BF16)
        wd_s[...] = wd_ref[...].astype(BF16)

    @pl.when(nvalid == 0)
    def _():
        o_ref[...] = jnp.zeros_like(o_ref)

    @pl.when(nvalid > 0)
    def _():
        x = x_ref[...].astype(BF16)
        gu = jnp.dot(x, wgu_s[...], preferred_element_type=F32) + bgu_ref[...]
        acts = []
        for g in range(f // PERM_W):
            glu = jnp.minimum(gu[:, g * PERM_W:g * PERM_W + LANES], SWIGLU_LIMIT)
            lin = jnp.clip(gu[:, g * PERM_W + LANES:(g + 1) * PERM_W], -SWIGLU_LIMIT, SWIGLU_LIMIT)
            acts.append((glu * _sigmoid(SWIGLU_ALPHA * glu) * (lin + 1.0)).astype(BF16))
        act = jnp.concatenate(acts, axis=-1)
        o_ref[...] = jnp.dot(act, wd_s[...], preferred_element_type=F32) + bd_ref[...]


def _experts(block_expert, block_nvalid, n_used, xs, w_gate_up, bgu_perm, w_down, b_down3, perm):
    n_rows, d = xs.shape
    tm = MOE_TM
    nb = n_rows // tm
    e, _, f = w_gate_up.shape
    de = w_down.shape[1]
    blk = lambda b, be, nv, nu: (jnp.minimum(b, jnp.maximum(nu[0] - 1, 0)), 0)
    return pl.pallas_call(
        _experts_body,
        out_shape=jax.ShapeDtypeStruct((n_rows, d), F32),
        grid_spec=pltpu.PrefetchScalarGridSpec(
            num_scalar_prefetch=3,
            grid=(nb,),
            in_specs=[pl.BlockSpec((tm, d), blk),
                      pl.BlockSpec((None, d, f), lambda b, be, nv, nu: (be[b], 0, 0)),
                      pl.BlockSpec((None, 1, f), lambda b, be, nv, nu: (be[b], 0, 0)),
                      pl.BlockSpec((None, de, d), lambda b, be, nv, nu: (be[b], 0, 0)),
                      pl.BlockSpec((None, 1, d), lambda b, be, nv, nu: (be[b], 0, 0)),
                      pl.BlockSpec((PERM_W, PERM_W), lambda b, be, nv, nu: (0, 0))],
            out_specs=pl.BlockSpec((tm, d), lambda b, be, nv, nu: (b, 0)),
            scratch_shapes=[pltpu.VMEM((d, f), BF16), pltpu.VMEM((de, d), BF16)]),
        compiler_params=pltpu.CompilerParams(dimension_semantics=("arbitrary",),
                                             vmem_limit_bytes=VMEM_LIMIT),
        name="experts",
    )(block_expert, block_nvalid, n_used, xs, w_gate_up, bgu_perm, w_down, b_down3, perm)


COMB_T = 256


def _combine_body(alpha, cur_ref, nxt_ref, x1_ref, w_ref, gate_ref, lng_ref, lnb_ref,
                  ys_ref, o_ref, gbuf_ref, sems):
    t = COMB_T
    i = pl.program_id(0)
    slot = i & 1

    def copy(src_ref, s, r, k):
        return pltpu.make_async_copy(ys_ref.at[pl.ds(src_ref[0, 0, r * TOP_K + k], 1), :],
                                     gbuf_ref.at[s, k, pl.ds(r, 1), :], sems.at[s])

    def issue(src_ref, s):
        def body(r, _):
            for k in range(TOP_K):
                copy(src_ref, s, r, k).start(priority=k % 2)
            return 0
        lax.fori_loop(0, t, body, 0, unroll=2)

    @pl.when(i == 0)
    def _():
        issue(cur_ref, slot)

    @pl.when(i + 1 < pl.num_programs(0))
    def _():
        issue(nxt_ref, 1 - slot)

    def wait(r, _):
        for k in range(TOP_K):
            copy(cur_ref, slot, r, k).wait()
        return 0

    lax.fori_loop(0, t, wait, 0, unroll=8)
    w = w_ref[...]
    y = w[:, 0:1] * gbuf_ref[slot, 0]
    for k in range(1, TOP_K):
        y = y + w[:, k:k + 1] * gbuf_ref[slot, k]
    r = alpha * x1_ref[...] + gate_ref[0:1, :] * y
    o_ref[...] = _layer_norm_rows(r, lng_ref[...], lnb_ref[...])


def _combine(alpha, dest3, x1, w_top, mods, lng, lnb, ys):
    s, d = x1.shape
    t = COMB_T
    n = s // t
    return pl.pallas_call(
        functools.partial(_combine_body, alpha),
        out_shape=jax.ShapeDtypeStruct((s, d), F32),
        grid=(n,),
        in_specs=[pl.BlockSpec((1, 1, t * TOP_K), lambda i: (i, 0, 0), memory_space=pltpu.SMEM),
                  pl.BlockSpec((1, 1, t * TOP_K), lambda i: (jnp.minimum(i + 1, n - 1), 0, 0),
                               memory_space=pltpu.SMEM),
                  pl.BlockSpec((t, d), lambda i: (i, 0)),
                  pl.BlockSpec((t, LANES), lambda i: (i, 0)),
                  pl.BlockSpec((SUBLANES, d), lambda i: (0, 5)),
                  pl.BlockSpec((1, d), lambda i: (0, 0)),
                  pl.BlockSpec((1, d), lambda i: (0, 0)),
                  pl.BlockSpec(memory_space=pl.ANY)],
        out_specs=pl.BlockSpec((t, d), lambda i: (i, 0)),
        scratch_shapes=[pltpu.VMEM((2, TOP_K, t, d), F32), pltpu.SemaphoreType.DMA((2,))],
        compiler_params=pltpu.CompilerParams(dimension_semantics=("arbitrary",),
                                             vmem_limit_bytes=VMEM_LIMIT),
        name="combine",
    )(dest3, dest3, x1, w_top, mods, lng, lnb, ys)


def _perm_matrix():
    p = np.zeros((PERM_W, PERM_W), np.float32)
    for c in range(PERM_W):
        p[c, (c // 2) + (LANES if c % 2 else 0)] = 1.0
    return jnp.asarray(p, BF16)


def _pad_lanes(v, offset):
    return jnp.zeros((1, LANES), F32).at[0, offset:offset + v.shape[0]].set(v.astype(F32))


def _layer(x2, silu_in, positions, layer, w_ada, b_ada, w_in, conv_w, dn_a_log, dn_dt_bias,
           dn_norm_w, w_dn_proj, lambda_q1, lambda_k1, lambda_q2, lambda_k2, da_norm_w,
           w_da_proj, w_o, ln1_g, ln1_b, w_router, b_router, w_gate_up, b_gate_up, w_down,
           b_down, ln2_g, ln2_b, alpha):
    s, d = x2.shape
    mods = _mods(silu_in, w_ada, b_ada[None, :])

    offs = np.cumsum([0, DN_WIDTH, DN_WIDTH, DN_WIDTH, DN_WIDTH, DN_HEADS, DN_HEADS,
                      DA_WIDTH, DA_WIDTH, DA_WIDTH, d, d])
    seg = lambda n: w_in[:, offs[n]:offs[n + 1]]
    w_re = jnp.concatenate(
        [seg(9), seg(10), seg(0), seg(1), seg(2), seg(3), seg(6), seg(7), seg(8), seg(4), seg(5),
         jnp.zeros((d, NP_COLS - COL_BA - 2 * DN_HEADS), w_in.dtype)], axis=1).astype(BF16)
    proj = _inproj(x2, mods, w_re)

    half = DA_HEAD_DIM // 2
    inv_freq = ROPE_THETA ** (-jnp.arange(half, dtype=F32) / half)
    invf = jnp.tile(inv_freq, LANES // half)[None, :]
    sgn = jnp.tile(jnp.concatenate([-jnp.ones((half,), F32), jnp.ones((half,), F32)]),
                   LANES // DA_HEAD_DIM)[None, :]
    qt4, k12, vt3 = _attnprep(positions.reshape(s, 1), invf, sgn, proj)
    lam_init = 0.8 - 0.6 * math.exp(-0.3 * layer)
    o_da = _attn(lam_init, qt4, k12, vt3, lambda_q1[None, :], lambda_k1[None, :],
                 lambda_q2[None, :], lambda_k2[None, :], da_norm_w[:, None])

    o_dn = _deltanet(proj, conv_w, _pad_lanes(dn_a_log, DN_HEADS), _pad_lanes(dn_dt_bias, DN_HEADS))

    wr_pad = jnp.zeros((d, LANES), F32).at[:, :N_EXPERTS].set(w_router)
    x1, u2, logits = _post(alpha, x2, o_dn, proj, o_da, mods, dn_norm_w[None, :],
                           w_dn_proj.astype(BF16), w_da_proj.astype(BF16), w_o.astype(BF16),
                           ln1_g[None, :], ln1_b[None, :], wr_pad, _pad_lanes(b_router, 0))

    idx_l, rank_l, w_top, counts_l = _route(logits)
    counts = counts_l[0, :N_EXPERTS]
    tm = MOE_TM
    nblk_e = (counts + tm - 1) // tm
    blk_end = jnp.cumsum(nblk_e)
    blk_start = blk_end - nblk_e
    pstart = (blk_start * tm).astype(I32)
    nb = (s * TOP_K) // tm + N_EXPERTS
    n_used = blk_end[-1]
    bids = jnp.arange(nb, dtype=I32)
    last_e = jnp.max(jnp.where(counts > 0, jnp.arange(N_EXPERTS, dtype=I32), 0))
    be = jnp.sum((bids[:, None] >= blk_end[None, :]).astype(I32), axis=1)
    be = jnp.where(bids < n_used, jnp.minimum(be, N_EXPERTS - 1), last_e).astype(I32)
    nvalid = jnp.clip(counts[be] - (bids - blk_start[be]) * tm, 0, tm)
    nvalid = jnp.where(bids < n_used, nvalid, 0).astype(I32)
    idx4 = idx_l[:, :TOP_K]
    e_ids = jnp.arange(N_EXPERTS, dtype=I32)
    dest = jnp.sum(jnp.where(idx4[:, :, None] == e_ids, pstart, 0), axis=-1) + rank_l[:, :TOP_K]
    dest = dest.astype(I32)

    n_used = n_used.astype(I32)[None]
    xs = _dispatch(pstart, counts.astype(I32), n_used,
                   dest.reshape(s // DISP_T, 1, DISP_T * TOP_K), u2, nb * tm)
    e, _, f = w_gate_up.shape
    bgu_perm = b_gate_up.reshape(e, f // PERM_W, LANES, 2).transpose(0, 1, 3, 2).reshape(e, 1, f)
    ys = _experts(be, nvalid, n_used, xs, w_gate_up, bgu_perm, w_down,
                  b_down[:, None, :], _perm_matrix())
    return _combine(alpha, dest.reshape(s // COMB_T, 1, COMB_T * TOP_K), x1, w_top, mods,
                    ln2_g[None, :], ln2_b[None, :], ys)


def kernel(x, c, positions, w_ada, b_ada, w_in, conv_w, dn_a_log, dn_dt_bias, dn_norm_w, w_dn_proj, lambda_q1, lambda_k1, lambda_q2, lambda_k2, da_norm_w, w_da_proj, w_o, ln1_g, ln1_b, w_router, b_router, w_gate_up, b_gate_up, w_down, b_down, ln2_g, ln2_b):
    b, s, d = x.shape
    assert b == 1, "kernel is written for a single sequence"
    depth = w_ada.shape[0]
    alpha = (2.0 * depth) ** 0.25
    c8 = jnp.zeros((SUBLANES, d), F32).at[0:1, :].set(c.astype(F32))
    x2 = x.reshape(s, d)
    for l in range(depth):
        x2 = _layer(x2, c8, positions, l, w_ada[l], b_ada[l], w_in[l], conv_w[l], dn_a_log[l],
                    dn_dt_bias[l], dn_norm_w[l], w_dn_proj[l], lambda_q1[l], lambda_k1[l],
                    lambda_q2[l], lambda_k2[l], da_norm_w[l], w_da_proj[l], w_o[l], ln1_g[l],
                    ln1_b[l], w_router[l], b_router[l], w_gate_up[l], b_gate_up[l], w_down[l],
                    b_down[l], ln2_g[l], ln2_b[l], alpha)
    return x2.reshape(b, s, d)
```

```python
import functools
import math

import numpy as np
import jax
import jax.numpy as jnp
from jax import lax
from jax.experimental import pallas as pl
from jax.experimental.pallas import tpu as pltpu

F32 = jnp.float32
BF16 = jnp.bfloat16
I32 = jnp.int32
HIGHEST = lax.Precision.HIGHEST

DN_HEADS = 4
DN_HEAD_DIM = 128
DN_WIDTH = DN_HEADS * DN_HEAD_DIM
CONV_WIDTH = 4
DN_CHUNK = 64
DA_HEADS = 4
DA_HEAD_DIM = 64
DA_VDIM = 2 * DA_HEAD_DIM
DA_WIDTH = DA_HEADS * DA_VDIM
ROPE_THETA = 10000.0
N_EXPERTS = 32
TOP_K = 4
SWIGLU_LIMIT = 7.0
SWIGLU_ALPHA = 1.702
LN_EPS = 1e-5
L2_EPS = 1e-6

LANES = 128
SUBLANES = 8
VMEM_LIMIT = 56 * 1024 * 1024

COL_GDN, COL_GDA, COL_DQ, COL_DK, COL_DV, COL_Z, COL_AQ, COL_AK, COL_AV, COL_BA = (
    0, 1024, 2048, 2560, 3072, 3584, 4096, 4608, 5120, 5632)
NP_COLS = 5760

MOE_TM = 256


def _sigmoid(x):
    return 1.0 / (1.0 + jnp.exp(-x))


def _silu(x):
    return x * _sigmoid(x)


def _bdot(a, b):
    return jnp.dot(a.astype(BF16), b.astype(BF16), preferred_element_type=F32)


def _bdot_nt(a, b):
    return lax.dot_general(a.astype(BF16), b.astype(BF16), (((1,), (1,)), ((), ())),
                           preferred_element_type=F32)


def _bdot_tn(a, b):
    return lax.dot_general(a.astype(BF16), b.astype(BF16), (((0,), (0,)), ((), ())),
                           preferred_element_type=F32)


def _layer_norm_rows(r, g, b):
    mu = jnp.mean(r, axis=-1, keepdims=True)
    d = r - mu
    var = jnp.mean(d * d, axis=-1, keepdims=True)
    return d * lax.rsqrt(var + LN_EPS) * g + b


def _mods_body(c_ref, w_ref, b_ref, o_ref):
    c = c_ref[...]
    o_ref[...] = jnp.dot(_silu(c), w_ref[...], preferred_element_type=F32,
                         precision=HIGHEST) + b_ref[...]


def _mods(c8, w_ada, b_ada):
    d, n = w_ada.shape
    tn = 1536
    return pl.pallas_call(
        _mods_body,
        out_shape=jax.ShapeDtypeStruct((SUBLANES, n), F32),
        grid=(n // tn,),
        in_specs=[pl.BlockSpec((SUBLANES, d), lambda j: (0, 0)),
                  pl.BlockSpec((d, tn), lambda j: (0, j)),
                  pl.BlockSpec((1, tn), lambda j: (0, j))],
        out_specs=pl.BlockSpec((SUBLANES, tn), lambda j: (0, j)),
        compiler_params=pltpu.CompilerParams(dimension_semantics=("arbitrary",),
                                             vmem_limit_bytes=VMEM_LIMIT),
        name="mods",
    )(c8, w_ada, b_ada)


def _inproj_body(x_ref, shift_ref, scale_ref, w_ref, o_ref, u_ref):
    @pl.when(pl.program_id(1) == 0)
    def _():
        u = x_ref[...] * (1.0 + scale_ref[0:1, :]) + shift_ref[0:1, :]
        u_ref[...] = u.astype(BF16)

    o_ref[...] = jnp.dot(u_ref[...], w_ref[...], preferred_element_type=F32)


def _inproj(x2, mods, w_re):
    s, d = x2.shape
    tm = min(s, 1024)
    tn = 1152
    return pl.pallas_call(
        _inproj_body,
        out_shape=jax.ShapeDtypeStruct((s, NP_COLS), F32),
        grid=(s // tm, NP_COLS // tn),
        in_specs=[pl.BlockSpec((tm, d), lambda i, j: (i, 0)),
                  pl.BlockSpec((SUBLANES, d), lambda i, j: (0, 0)),
                  pl.BlockSpec((SUBLANES, d), lambda i, j: (0, 1)),
                  pl.BlockSpec((d, tn), lambda i, j: (0, j))],
        out_specs=pl.BlockSpec((tm, tn), lambda i, j: (i, j)),
        scratch_shapes=[pltpu.VMEM((tm, d), BF16)],
        compiler_params=pltpu.CompilerParams(dimension_semantics=("arbitrary", "arbitrary"),
                                             vmem_limit_bytes=VMEM_LIMIT),
        name="inproj",
    )(x2, mods, mods, w_re)


ATT_T = 512


def _attnprep_body(pos_ref, invf_ref, sgn_ref, aq_ref, ak_ref, av_ref, qt_ref, k_ref, vt_ref):
    ang = pos_ref[...].astype(F32) * invf_ref[...]
    cos = jnp.cos(ang)
    sin = jnp.sin(ang) * sgn_ref[...]
    t = ang.shape[0]
    lane = lax.broadcasted_iota(I32, (t, LANES), 1)
    first_half = (lane & (DA_HEAD_DIM - 1)) < (DA_HEAD_DIM // 2)
    low_map = lane < DA_HEAD_DIM

    def rope(x):
        swapped = jnp.where(first_half, pltpu.roll(x, LANES - DA_HEAD_DIM // 2, 1),
                            pltpu.roll(x, DA_HEAD_DIM // 2, 1))
        return x * cos + swapped * sin

    scale = DA_HEAD_DIM ** -0.5 * math.log2(math.e)
    for h in range(DA_HEADS):
        cols = slice(h * LANES, (h + 1) * LANES)
        q = rope(aq_ref[:, cols]) * scale
        qt_ref[h, 0, :, 0:t] = jnp.where(low_map, q, 0.0).T.astype(BF16)
        qt_ref[h, 0, :, t:2 * t] = jnp.where(low_map, 0.0, q).T.astype(BF16)
        k_ref[:, cols] = rope(ak_ref[:, cols]).astype(BF16)
        vt_ref[h, 0] = av_ref[:, cols].T.astype(BF16)


def _attnprep(pos_col, invf, sgn, proj):
    s = proj.shape[0]
    t = ATT_T
    nt = s // t
    w = DA_WIDTH
    return pl.pallas_call(
        _attnprep_body,
        out_shape=(jax.ShapeDtypeStruct((DA_HEADS, nt, LANES, 2 * t), BF16),
                   jax.ShapeDtypeStruct((s, w), BF16),
                   jax.ShapeDtypeStruct((DA_HEADS, nt, LANES, t), BF16)),
        grid=(nt,),
        in_specs=[pl.BlockSpec((t, 1), lambda i: (i, 0)),
                  pl.BlockSpec((1, LANES), lambda i: (0, 0)),
                  pl.BlockSpec((1, LANES), lambda i: (0, 0)),
                  pl.BlockSpec((t, w), lambda i: (i, COL_AQ // w)),
                  pl.BlockSpec((t, w), lambda i: (i, COL_AK // w)),
                  pl.BlockSpec((t, w), lambda i: (i, COL_AV // w))],
        out_specs=(pl.BlockSpec((DA_HEADS, 1, LANES, 2 * t), lambda i: (0, i, 0, 0)),
                   pl.BlockSpec((t, w), lambda i: (i, 0)),
                   pl.BlockSpec((DA_HEADS, 1, LANES, t), lambda i: (0, i, 0, 0))),
        compiler_params=pltpu.CompilerParams(dimension_semantics=("arbitrary",),
                                             vmem_limit_bytes=VMEM_LIMIT),
        name="attnprep",
    )(pos_col, invf, sgn, proj, proj, proj)


def _attn_body(lam_init, qt_ref, k_ref, vt_ref, lq1_ref, lk1_ref, lq2_ref, lk2_ref,
               nw_ref, o_ref, acc_ref, s0_ref, s1_ref):
    t = ATT_T
    i = pl.program_id(1)
    qt = qt_ref[...]

    def scores(j, dst):
        start = pl.multiple_of(j * t, t)
        dst[...] = jnp.dot(k_ref[pl.ds(start, t), :], qt, preferred_element_type=F32)

    def update(j, src, m_old, l_old, masked):
        st = src[...]
        if masked:
            row = lax.broadcasted_iota(I32, (t, 2 * t), 0)
            col = lax.broadcasted_iota(I32, (t, 2 * t), 1) & (t - 1)
            st = jnp.where(row <= col, st, -jnp.inf)
        m_new = jnp.maximum(m_old, jnp.max(st, axis=0, keepdims=True))
        alpha = jnp.exp2(m_old - m_new)
        p = jnp.exp2(st - m_new)
        l_new = alpha * l_old + jnp.sum(p, axis=0, keepdims=True)
        acc_ref[...] = alpha * acc_ref[...] + jnp.dot(vt_ref[j], p.astype(BF16),
                                                      preferred_element_type=F32)
        return m_new, l_new

    def finalize(l):
        lam = (jnp.exp(jnp.sum(lq1_ref[...] * lk1_ref[...], axis=-1, keepdims=True))
               - jnp.exp(jnp.sum(lq2_ref[...] * lk2_ref[...], axis=-1, keepdims=True)) + lam_init)
        ot = acc_ref[:, 0:t] / l[:, 0:t] - lam * (acc_ref[:, t:2 * t] / l[:, t:2 * t])
        ms = jnp.mean(ot * ot, axis=0, keepdims=True)
        ot = ot * lax.rsqrt(ms + LN_EPS) * nw_ref[...] * (1.0 - lam_init)
        o_ref[...] = ot.T

    scores(0, s0_ref)
    acc_ref[...] = jnp.zeros_like(acc_ref)

    def pair(j, carry):
        scores(j + 1, s1_ref)
        carry = update(j, s0_ref, *carry, False)
        scores(j + 2, s0_ref)
        return update(j + 1, s1_ref, *carry, False)

    def quad(p, carry):
        return pair(4 * p + 2, pair(4 * p, carry))

    n_quad = lax.shift_right_logical(i, 2)
    carry = lax.fori_loop(0, n_quad, quad,
                          (jnp.full((1, 2 * t), -jnp.inf, F32), jnp.zeros((1, 2 * t), F32)))
    n_pair = lax.shift_right_logical(i & 3, 1)
    m, l = lax.fori_loop(0, n_pair, lambda p, c: pair(4 * n_quad + 2 * p, c), carry)

    @pl.when((i & 1) == 0)
    def _():
        finalize(update(i, s0_ref, m, l, True)[1])

    @pl.when((i & 1) == 1)
    def _():
        scores(i, s1_ref)
        mid = update(i - 1, s0_ref, m, l, False)
        finalize(update(i, s1_ref, *mid, True)[1])


def _attn(lam_init, qt4, k12, vt3, lq1, lk1, lq2, lk2, nw_col):
    s = k12.shape[0]
    t = ATT_T
    nt = s // t
    small = pl.BlockSpec((1, DA_HEAD_DIM), lambda h, i: (0, 0))
    return pl.pallas_call(
        functools.partial(_attn_body, lam_init),
        out_shape=jax.ShapeDtypeStruct((s, DA_WIDTH), F32),
        grid=(DA_HEADS, nt),
        in_specs=[pl.BlockSpec((None, None, LANES, 2 * t), lambda h, i: (h, i, 0, 0)),
                  pl.BlockSpec((s, LANES), lambda h, i: (0, h)),
                  pl.BlockSpec((None, nt, LANES, t), lambda h, i: (h, 0, 0, 0)),
                  small, small, small, small,
                  pl.BlockSpec((DA_VDIM, 1), lambda h, i: (0, 0))],
        out_specs=pl.BlockSpec((t, LANES), lambda h, i: (i, h)),
        scratch_shapes=[pltpu.VMEM((LANES, 2 * t), F32),
                        pltpu.VMEM((t, 2 * t), F32), pltpu.VMEM((t, 2 * t), F32)],
        compiler_params=pltpu.CompilerParams(dimension_semantics=("arbitrary", "arbitrary"),
                                             vmem_limit_bytes=VMEM_LIMIT),
        name="attn",
    )(qt4, k12, vt3, lq1, lk1, lq2, lk2, nw_col)


DN_NC = 4
DN_TT = DN_NC * DN_CHUNK


def _tri_inverse(a_list, ii, jj, eye):
    zip_dot = lambda xs, ys: [_bdot(x, y) for x, y in zip(xs, ys)]
    blk8 = (ii >> 3) == (jj >> 3)
    n = [jnp.where(blk8, -a, 0.0) for a in a_list]
    n2 = zip_dot(n, n)
    n4 = zip_dot(n2, n2)
    nn2 = zip_dot(n, n2)
    p = [eye + x + y + z for x, y, z in zip(n, n2, nn2)]
    pn4 = zip_dot(p, n4)
    d = [x + y for x, y in zip(p, pn4)]
    for sh in (3, 4, 5):
        bi, bj = ii >> sh, jj >> sh
        below = ((bi & 1) == 1) & (bj == bi - 1)
        lo = [jnp.where(below, a, 0.0) for a in a_list]
        dld = zip_dot(zip_dot(d, lo), d)
        d = [x - y for x, y in zip(d, dld)]
    return d


def _cumsum_rows(x, row):
    shift = 1
    while shift < x.shape[0]:
        x = x + jnp.where(row >= shift, pltpu.roll(x, shift, 0), 0.0)
        shift *= 2
    return x


def _dn_body(q_ref, k_ref, v_ref, ba_ref, cw_ref, alog_ref, dtb_ref, o_ref, xbuf_ref, state_ref):
    tt = DN_TT
    c = DN_CHUNK
    nh = DN_HEADS
    dh = DN_HEAD_DIM
    r = nh * c

    @pl.when(pl.program_id(0) == 0)
    def _():
        xbuf_ref[:, 0:SUBLANES, :] = jnp.zeros((3, SUBLANES, DN_WIDTH), F32)
        state_ref[...] = jnp.zeros_like(state_ref)

    ys = []
    for n, ref in enumerate((q_ref, k_ref, v_ref)):
        cur = ref[...]
        xbuf_ref[n, SUBLANES:SUBLANES + tt, :] = cur
        acc = None
        for j in range(CONV_WIDTH):
            off = SUBLANES - (CONV_WIDTH - 1) + j
            term = cw_ref[j:j + 1, n * DN_WIDTH:(n + 1) * DN_WIDTH] * xbuf_ref[n, off:off + tt, :]
            acc = term if acc is None else acc + term
        xbuf_ref[n, 0:SUBLANES, :] = cur[tt - SUBLANES:tt, :]
        ys.append(_silu(acc))
    yq, yk, yv = ys

    ba = ba_ref[...]
    beta_full = _sigmoid(ba)
    g_full = -jnp.exp(alog_ref[...]) * jax.nn.softplus(ba + dtb_ref[...])

    qn, kn, vv = [], [], []
    for h in range(nh):
        cols = slice(h * dh, (h + 1) * dh)
        qh, kh = yq[:, cols], yk[:, cols]
        qn.append(qh * lax.rsqrt(jnp.sum(qh * qh, axis=-1, keepdims=True) + L2_EPS) * (dh ** -0.5))
        kn.append(kh * lax.rsqrt(jnp.sum(kh * kh, axis=-1, keepdims=True) + L2_EPS))
        vv.append(yv[:, cols])

    ii = lax.broadcasted_iota(I32, (r, r), 0)
    jj = lax.broadcasted_iota(I32, (r, r), 1)
    eye = (ii == jj).astype(F32)
    same_head = (ii >> 6) == (jj >> 6)
    incl = same_head & (ii >= jj)
    strict = same_head & (ii > jj)
    row_c = lax.broadcasted_iota(I32, (c, LANES), 0)
    chunks = range(DN_NC)

    def head_cols(x, lane0):
        return jnp.concatenate([x[:, lane0 + h:lane0 + h + 1] for h in range(nh)], axis=0)

    def head_rows(x, lane0):
        xt = jnp.concatenate([x, jnp.zeros_like(x)], axis=0).T
        return jnp.concatenate([xt[lane0 + h:lane0 + h + 1, 0:c] for h in range(nh)], axis=1)

    def stack(parts, rows):
        return jnp.concatenate([p[rows, :] for p in parts], axis=0)

    rows_of = [slice(ci * c, (ci + 1) * c) for ci in chunks]
    cg = [_cumsum_rows(g_full[rows, :], row_c) for rows in rows_of]
    tot = [x[c - 1:c, :] for x in cg]
    eg_st = [head_cols(jnp.exp(x), nh) for x in cg]
    erev_st = [head_cols(jnp.exp(t - x), nh) for x, t in zip(cg, tot)]
    etot = [jnp.exp(t) for t in tot]
    beta_st = [head_cols(beta_full[rows, :], 0) for rows in rows_of]
    decay = [jnp.exp(jnp.where(incl, head_cols(x, nh) - head_rows(x, nh), -jnp.inf)) for x in cg]
    k_st = [stack(kn, rows) for rows in rows_of]
    q_st = [stack(qn, rows) for rows in rows_of]
    v_st = [stack(vv, rows) for rows in rows_of]
    kk = [_bdot_nt(k, k) for k in k_st]
    qk = [_bdot_nt(q, k) * dc for q, k, dc in zip(q_st, k_st, decay)]
    a = [jnp.where(strict, b * dc * x, 0.0) for b, dc, x in zip(beta_st, decay, kk)]
    tinv = _tri_inverse(a, ii, jj, eye)
    rhs = [jnp.concatenate([b * v, (b * e) * k], axis=1)
           for b, e, k, v in zip(beta_st, eg_st, k_st, v_st)]
    uw = [_bdot(t, x) for t, x in zip(tinv, rhs)]
    qw = [_bdot(x, y) for x, y in zip(qk, uw)]
    o_loc = [x[:, 0:dh] for x in qw]
    q_eff = [q * e - x[:, dh:2 * dh] for q, e, x in zip(q_st, eg_st, qw)]
    k_dec = [k * e for k, e in zip(k_st, erev_st)]
    pb = [[_bdot_tn(kd[h * c:(h + 1) * c, :], x[h * c:(h + 1) * c, :]) for h in range(nh)]
          for kd, x in zip(k_dec, uw)]

    states = [state_ref[h] for h in range(nh)]
    for ci in chunks:
        for h in range(nh):
            hrows = slice(h * c, (h + 1) * c)
            lhs = jnp.concatenate([pb[ci][h][:, dh:2 * dh], q_eff[ci][hrows, :]], axis=0)
            ps = _bdot(lhs, states[h])
            gl = nh + h
            states[h] = states[h] * etot[ci][:, gl:gl + 1] - ps[0:dh, :] + pb[ci][h][:, 0:dh]
            o_ref[rows_of[ci], h * dh:(h + 1) * dh] = ps[dh:dh + c, :] + o_loc[ci][hrows, :]
    for h in range(nh):
        state_ref[h] = states[h]


def _deltanet(proj, conv_w, alog_row, dtb_row):
    s = proj.shape[0]
    tt = DN_TT
    w = DN_WIDTH
    return pl.pallas_call(
        _dn_body,
        out_shape=jax.ShapeDtypeStruct((s, w), F32),
        grid=(s // tt,),
        in_specs=[pl.BlockSpec((tt, w), lambda i: (i, COL_DQ // w)),
                  pl.BlockSpec((tt, w), lambda i: (i, COL_DK // w)),
                  pl.BlockSpec((tt, w), lambda i: (i, COL_DV // w)),
                  pl.BlockSpec((tt, LANES), lambda i: (i, COL_BA // LANES)),
                  pl.BlockSpec((CONV_WIDTH, 3 * w), lambda i: (0, 0)),
                  pl.BlockSpec((1, LANES), lambda i: (0, 0)),
                  pl.BlockSpec((1, LANES), lambda i: (0, 0))],
        out_specs=pl.BlockSpec((tt, w), lambda i: (i, 0)),
        scratch_shapes=[pltpu.VMEM((3, tt + SUBLANES, w), F32),
                        pltpu.VMEM((DN_HEADS, DN_HEAD_DIM, DN_HEAD_DIM), F32)],
        compiler_params=pltpu.CompilerParams(dimension_semantics=("arbitrary",),
                                             vmem_limit_bytes=VMEM_LIMIT),
        name="deltanet",
    )(proj, proj, proj, proj, conv_w, alog_row, dtb_row)


POST_SUB = 128

def _post_body(alpha, x_ref, odn_ref, z_ref, oda_ref, gdn_ref, gda_ref, gate_ref, scale_ref,
               shift_ref, dnw_ref, wdn_ref, wda_ref, wo_ref, lng_ref, lnb_ref, wr_ref, br_ref,
               x1_ref, u2_ref, lg_ref):
    for r0 in range(0, x_ref.shape[0], POST_SUB):
        rows = slice(r0, r0 + POST_SUB)
        odn = odn_ref[rows, :]
        z = z_ref[rows, :]
        parts = []
        for h in range(DN_HEADS):
            cols = slice(h * DN_HEAD_DIM, (h + 1) * DN_HEAD_DIM)
            o = odn[:, cols]
            o = o * lax.rsqrt(jnp.mean(o * o, axis=-1, keepdims=True) + LN_EPS) * dnw_ref[...]
            parts.append((o * _silu(z[:, cols])).astype(BF16))
        y_dn = jnp.dot(jnp.concatenate(parts, axis=-1), wdn_ref[...], preferred_element_type=F32)
        y_da = jnp.dot(oda_ref[rows, :].astype(BF16), wda_ref[...], preferred_element_type=F32)
        merged = _sigmoid(gdn_ref[rows, :]) * y_dn + _sigmoid(gda_ref[rows, :]) * y_da
        mix = jnp.dot(merged.astype(BF16), wo_ref[...], preferred_element_type=F32)
        r = alpha * x_ref[rows, :] + gate_ref[0:1, :] * mix
        x1 = _layer_norm_rows(r, lng_ref[...], lnb_ref[...])
        x1_ref[rows, :] = x1
        u2 = x1 * (1.0 + scale_ref[0:1, :]) + shift_ref[0:1, :]
        u2_ref[rows, :] = u2
        lg_ref[rows, :] = jnp.dot(u2, wr_ref[...], preferred_element_type=F32,
                                  precision=HIGHEST) + br_ref[...]


def _post(alpha, x2, o_dn, proj, o_da, mods, dnw, wdn, wda, wo, lng, lnb, wr_pad, br_pad):
    s, d = x2.shape
    tm = min(s, 512)
    row = lambda i: (i, 0)
    const = lambda i: (0, 0)
    return pl.pallas_call(
        functools.partial(_post_body, alpha),
        out_shape=(jax.ShapeDtypeStruct((s, d), F32),
                   jax.ShapeDtypeStruct((s, d), F32),
                   jax.ShapeDtypeStruct((s, LANES), F32)),
        grid=(s // tm,),
        in_specs=[pl.BlockSpec((tm, d), row),
                  pl.BlockSpec((tm, DN_WIDTH), row),
                  pl.BlockSpec((tm, DN_WIDTH), lambda i: (i, COL_Z // DN_WIDTH)),
                  pl.BlockSpec((tm, DA_WIDTH), row),
                  pl.BlockSpec((tm, d), lambda i: (i, COL_GDN // d)),
                  pl.BlockSpec((tm, d), lambda i: (i, COL_GDA // d)),
                  pl.BlockSpec((SUBLANES, d), lambda i: (0, 2)),
                  pl.BlockSpec((SUBLANES, d), lambda i: (0, 4)),
                  pl.BlockSpec((SUBLANES, d), lambda i: (0, 3)),
                  pl.BlockSpec((1, DN_HEAD_DIM), const),
                  pl.BlockSpec((DN_WIDTH, d), const),
                  pl.BlockSpec((DA_WIDTH, d), const),
                  pl.BlockSpec((d, d), const),
                  pl.BlockSpec((1, d), const),
                  pl.BlockSpec((1, d), const),
                  pl.BlockSpec((d, LANES), const),
                  pl.BlockSpec((1, LANES), const)],
        out_specs=(pl.BlockSpec((tm, d), row), pl.BlockSpec((tm, d), row),
                   pl.BlockSpec((tm, LANES), row)),
        compiler_params=pltpu.CompilerParams(dimension_semantics=("arbitrary",),
                                             vmem_limit_bytes=VMEM_LIMIT),
        name="post",
    )(x2, o_dn, proj, o_da, proj, proj, mods, mods, mods, dnw, wdn, wda, wo, lng, lnb,
      wr_pad, br_pad)


ROUTE_T = 512


def _route_body(lg_ref, idx_ref, rank_ref, w_ref, cnt_ref, carry_ref):
    t = ROUTE_T

    @pl.when(pl.program_id(0) == 0)
    def _():
        carry_ref[...] = jnp.zeros_like(carry_ref)

    lane = lax.broadcasted_iota(I32, (t, LANES), 1)
    lane_f = lane.astype(F32)
    l = jnp.where(lane < N_EXPERTS, lg_ref[...], -jnp.inf)
    vals, idxs, sels = [], [], []
    for _ in range(TOP_K):
        mk = jnp.max(l, axis=-1, keepdims=True)
        ik = jnp.min(jnp.where(l == mk, lane_f, float(LANES)), axis=-1, keepdims=True)
        sel = lane_f == ik
        l = jnp.where(sel, -jnp.inf, l)
        vals.append(mk)
        idxs.append(ik)
        sels.append(sel)
    es = [jnp.exp(v - vals[0]) for v in vals]
    den = es[0] + es[1] + es[2] + es[3]
    onehot = jnp.zeros((t, LANES), F32)
    for sel in sels:
        onehot = jnp.where(sel, 1.0, onehot)
    ri = lax.broadcasted_iota(I32, (t, t), 0)
    ci = lax.broadcasted_iota(I32, (t, t), 1)
    before = (ci < ri).astype(BF16)
    rank_full = carry_ref[0:1, :] + jnp.dot(before, onehot.astype(BF16),
                                            preferred_element_type=F32)
    idx_out = jnp.zeros((t, LANES), F32)
    rank_out = jnp.zeros((t, LANES), F32)
    w_out = jnp.zeros((t, LANES), F32)
    for k in range(TOP_K):
        rk = jnp.sum(jnp.where(sels[k], rank_full, 0.0), axis=-1, keepdims=True)
        idx_out = jnp.where(lane == k, idxs[k], idx_out)
        rank_out = jnp.where(lane == k, rk, rank_out)
        w_out = jnp.where(lane == k, es[k] / den, w_out)
    idx_ref[...] = idx_out.astype(I32)
    rank_ref[...] = rank_out.astype(I32)
    w_ref[...] = w_out
    new = carry_ref[0:1, :] + jnp.sum(onehot, axis=0, keepdims=True)
    carry_ref[...] = jnp.broadcast_to(new, carry_ref.shape)
    cnt_ref[...] = jnp.broadcast_to(new, cnt_ref.shape).astype(I32)


def _route(logits):
    s = logits.shape[0]
    t = ROUTE_T
    row = pl.BlockSpec((t, LANES), lambda i: (i, 0))
    return pl.pallas_call(
        _route_body,
        out_shape=(jax.ShapeDtypeStruct((s, LANES), I32),
                   jax.ShapeDtypeStruct((s, LANES), I32),
                   jax.ShapeDtypeStruct((s, LANES), F32),
                   jax.ShapeDtypeStruct((SUBLANES, LANES), I32)),
        grid=(s // t,),
        in_specs=[row],
        out_specs=(row, row, row, pl.BlockSpec((SUBLANES, LANES), lambda i: (0, 0))),
        scratch_shapes=[pltpu.VMEM((SUBLANES, LANES), F32)],
        compiler_params=pltpu.CompilerParams(dimension_semantics=("arbitrary",),
                                             vmem_limit_bytes=VMEM_LIMIT),
        name="route",
    )(logits)


DISP_T = 512


def _dispatch_body(pstart_ref, cnt_ref, nused_ref, dest_ref, u_ref, xs_ref, zbuf_ref, sem, zsem):
    t = DISP_T
    tm = MOE_TM
    nb = xs_ref.shape[0] // tm

    @pl.when(pl.program_id(0) == 0)
    def _():
        zbuf_ref[...] = jnp.zeros_like(zbuf_ref)

        def pad_copy(e, r):
            return pltpu.make_async_copy(zbuf_ref.at[pl.ds(0, 1), :],
                                         xs_ref.at[pl.ds(pstart_ref[e] + cnt_ref[e] + r, 1), :], zsem)

        def blk_copy(b):
            return pltpu.make_async_copy(zbuf_ref, xs_ref.at[pl.ds(pl.multiple_of(b * tm, tm), tm), :],
                                         zsem)

        def per_expert(fn):
            def body(e, _):
                n_pad = (-cnt_ref[e]) & (tm - 1)
                lax.fori_loop(0, n_pad, lambda r, _: (fn(pad_copy(e, r)), 0)[1], 0)
                return 0
            lax.fori_loop(0, N_EXPERTS, body, 0)

        per_expert(lambda cp: cp.start())
        lax.fori_loop(nused_ref[0], nb, lambda b, _: (blk_copy(b).start(), 0)[1], 0)
        per_expert(lambda cp: cp.wait())
        lax.fori_loop(nused_ref[0], nb, lambda b, _: (blk_copy(b).wait(), 0)[1], 0)

    def copy(r, k):
        dest = dest_ref[0, 0, r * TOP_K + k]
        return pltpu.make_async_copy(u_ref.at[pl.ds(r, 1), :], xs_ref.at[pl.ds(dest, 1), :], sem)

    def start(r, _):
        for k in range(TOP_K):
            copy(r, k).start(priority=k % 2)
        return 0

    def wait(r, _):
        for k in range(TOP_K):
            copy(r, k).wait()
        return 0

    lax.fori_loop(0, t, start, 0, unroll=2)
    lax.fori_loop(0, t, wait, 0, unroll=8)


def _dispatch(pstart, counts, n_used, dest3, u2, n_rows):
    s, d = u2.shape
    t = DISP_T
    smem = lambda: pl.BlockSpec((1, 1, t * TOP_K), lambda i, *_: (i, 0, 0),
                                memory_space=pltpu.SMEM)
    return pl.pallas_call(
        _dispatch_body,
        out_shape=jax.ShapeDtypeStruct((n_rows, d), F32),
        grid_spec=pltpu.PrefetchScalarGridSpec(
            num_scalar_prefetch=3,
            grid=(s // t,),
            in_specs=[smem(), pl.BlockSpec((t, d), lambda i, *_: (i, 0))],
            out_specs=pl.BlockSpec(memory_space=pl.ANY),
            scratch_shapes=[pltpu.VMEM((MOE_TM, d), F32), pltpu.SemaphoreType.DMA,
                            pltpu.SemaphoreType.DMA]),
        compiler_params=pltpu.CompilerParams(dimension_semantics=("arbitrary",),
                                             vmem_limit_bytes=VMEM_LIMIT),
        name="dispatch",
    )(pstart, counts, n_used, dest3, u2)


PERM_W = 2 * LANES


def _experts_body(be_ref, nv_ref, nu_ref, eo_ref, nx_ref, x_ref, wgu_hbm, bgu_ref, wd_hbm, bd_ref,
                  perm_ref, o_ref, wgu_f, wd_f, wgu_s, wd_s, sems):
    b = pl.program_id(0)
    nvalid = nv_ref[b]
    e = be_ref[b]
    prev = be_ref[jnp.maximum(b - 1, 0)]
    changed = jnp.logical_or(b == 0, e != prev)
    f = wgu_s.shape[-1]

    def fetch(expert, slot):
        return (pltpu.make_async_copy(wgu_hbm.at[expert], wgu_f.at[slot], sems.at[slot]),
                pltpu.make_async_copy(wd_hbm.at[expert], wd_f.at[slot], sems.at[slot]))

    @pl.when(jnp.logical_and(changed, nvalid > 0))
    def _():
        slot = eo_ref[b] & 1

        @pl.when(b == 0)
        def _():
            for cp in fetch(e, slot):
                cp.start()

        for cp in fetch(e, slot):
            cp.wait()
        nxt = nx_ref[e]

        @pl.when(nxt >= 0)
        def _():
            for cp in fetch(nxt, 1 - slot):
                cp.start()

        for g in range(f // PERM_W):
            cols = slice(g * PERM_W, (g + 1) * PERM_W)
            wgu_s[:, cols] = jnp.dot(wgu_f[slot, :, cols].astype(BF16), perm_ref[...],
                                     preferred_element_type=F32).astype(BF16)
        wd_s[...] = wd_f[slot].astype(BF16)

    @pl.when(nvalid == 0)
    def _():
        o_ref[...] = jnp.zeros_like(o_ref)

    @pl.when(nvalid > 0)
    def _():
        x = x_ref[...].astype(BF16)
        gu = jnp.dot(x, wgu_s[...], preferred_element_type=F32) + bgu_ref[...]
        acts = []
        for g in range(f // PERM_W):
            glu = jnp.minimum(gu[:, g * PERM_W:g * PERM_W + LANES], SWIGLU_LIMIT)
            lin = jnp.clip(gu[:, g * PERM_W + LANES:(g + 1) * PERM_W], -SWIGLU_LIMIT, SWIGLU_LIMIT)
            acts.append((glu * _sigmoid(SWIGLU_ALPHA * glu) * (lin + 1.0)).astype(BF16))
        act = jnp.concatenate(acts, axis=-1)
        o_ref[...] = jnp.dot(act, wd_s[...], preferred_element_type=F32) + bd_ref[...]


def _experts(block_expert, block_nvalid, n_used, expert_ord, next_used, xs, w_gate_up, bgu_perm,
             w_down, b_down3, perm):
    n_rows, d = xs.shape
    tm = MOE_TM
    nb = n_rows // tm
    e, _, f = w_gate_up.shape
    de = w_down.shape[1]
    blk = lambda b, be, nv, nu, *_: (jnp.minimum(b, jnp.maximum(nu[0] - 1, 0)), 0)
    by_expert = lambda b, be, *_: (be[b], 0, 0)
    return pl.pallas_call(
        _experts_body,
        out_shape=jax.ShapeDtypeStruct((n_rows, d), F32),
        grid_spec=pltpu.PrefetchScalarGridSpec(
            num_scalar_prefetch=5,
            grid=(nb,),
            in_specs=[pl.BlockSpec((tm, d), blk),
                      pl.BlockSpec(memory_space=pl.ANY),
                      pl.BlockSpec((None, 1, f), by_expert),
                      pl.BlockSpec(memory_space=pl.ANY),
                      pl.BlockSpec((None, 1, d), by_expert),
                      pl.BlockSpec((PERM_W, PERM_W), lambda b, *_: (0, 0))],
            out_specs=pl.BlockSpec((tm, d), lambda b, *_: (b, 0)),
            scratch_shapes=[pltpu.VMEM((2, d, f), F32), pltpu.VMEM((2, de, d), F32),
                            pltpu.VMEM((d, f), BF16), pltpu.VMEM((de, d), BF16),
                            pltpu.SemaphoreType.DMA((2,))]),
        compiler_params=pltpu.CompilerParams(dimension_semantics=("arbitrary",),
                                             vmem_limit_bytes=VMEM_LIMIT),
        name="experts",
    )(block_expert, block_nvalid, n_used, expert_ord, next_used, xs, w_gate_up, bgu_perm, w_down,
      b_down3, perm)


COMB_T = 256


def _combine_body(alpha, cur_ref, nxt_ref, x1_ref, w_ref, gate_ref, lng_ref, lnb_ref,
                  ys_ref, o_ref, gbuf_ref, sems):
    t = COMB_T
    i = pl.program_id(0)
    slot = i & 1

    d = x1_ref.shape[1]

    def copy(src_ref, s, r, k):
        return pltpu.make_async_copy(ys_ref.at[pl.ds(src_ref[0, 0, r * TOP_K + k], 1), :],
                                     gbuf_ref.at[s, pl.ds(r, 1), pl.ds(k * d, d)], sems.at[s])

    def issue(src_ref, s):
        def body(r, _):
            for k in range(TOP_K):
                copy(src_ref, s, r, k).start(priority=k % 2)
            return 0
        lax.fori_loop(0, t, body, 0, unroll=2)

    @pl.when(i == 0)
    def _():
        issue(cur_ref, slot)

    @pl.when(i + 1 < pl.num_programs(0))
    def _():
        issue(nxt_ref, 1 - slot)

    def wait(r, _):
        for k in range(TOP_K):
            copy(cur_ref, slot, r, k).wait()
        return 0

    lax.fori_loop(0, t, wait, 0, unroll=8)
    w = w_ref[...]
    y = w[:, 0:1] * gbuf_ref[slot, :, 0:d]
    for k in range(1, TOP_K):
        y = y + w[:, k:k + 1] * gbuf_ref[slot, :, k * d:(k + 1) * d]
    r = alpha * x1_ref[...] + gate_ref[0:1, :] * y
    o_ref[...] = _layer_norm_rows(r, lng_ref[...], lnb_ref[...])


def _combine(alpha, dest3, x1, w_top, mods, lng, lnb, ys):
    s, d = x1.shape
    t = COMB_T
    n = s // t
    return pl.pallas_call(
        functools.partial(_combine_body, alpha),
        out_shape=jax.ShapeDtypeStruct((s, d), F32),
        grid=(n,),
        in_specs=[pl.BlockSpec((1, 1, t * TOP_K), lambda i: (i, 0, 0), memory_space=pltpu.SMEM),
                  pl.BlockSpec((1, 1, t * TOP_K), lambda i: (jnp.minimum(i + 1, n - 1), 0, 0),
                               memory_space=pltpu.SMEM),
                  pl.BlockSpec((t, d), lambda i: (i, 0)),
                  pl.BlockSpec((t, LANES), lambda i: (i, 0)),
                  pl.BlockSpec((SUBLANES, d), lambda i: (0, 5)),
                  pl.BlockSpec((1, d), lambda i: (0, 0)),
                  pl.BlockSpec((1, d), lambda i: (0, 0)),
                  pl.BlockSpec(memory_space=pl.ANY)],
        out_specs=pl.BlockSpec((t, d), lambda i: (i, 0)),
        scratch_shapes=[pltpu.VMEM((2, t, TOP_K * d), F32), pltpu.SemaphoreType.DMA((2,))],
        compiler_params=pltpu.CompilerParams(dimension_semantics=("arbitrary",),
                                             vmem_limit_bytes=VMEM_LIMIT),
        name="combine",
    )(dest3, dest3, x1, w_top, mods, lng, lnb, ys)


def _perm_matrix():
    p = np.zeros((PERM_W, PERM_W), np.float32)
    for c in range(PERM_W):
        p[c, (c // 2) + (LANES if c % 2 else 0)] = 1.0
    return jnp.asarray(p, BF16)


def _pad_lanes(v, offset):
    return jnp.zeros((1, LANES), F32).at[0, offset:offset + v.shape[0]].set(v.astype(F32))


def _layer(x2, silu_in, positions, layer, w_ada, b_ada, w_in, conv_w, dn_a_log, dn_dt_bias,
           dn_norm_w, w_dn_proj, lambda_q1, lambda_k1, lambda_q2, lambda_k2, da_norm_w,
           w_da_proj, w_o, ln1_g, ln1_b, w_router, b_router, w_gate_up, b_gate_up, w_down,
           b_down, ln2_g, ln2_b, alpha):
    s, d = x2.shape
    mods = _mods(silu_in, w_ada, b_ada[None, :])

    offs = np.cumsum([0, DN_WIDTH, DN_WIDTH, DN_WIDTH, DN_WIDTH, DN_HEADS, DN_HEADS,
                      DA_WIDTH, DA_WIDTH, DA_WIDTH, d, d])
    seg = lambda n: w_in[:, offs[n]:offs[n + 1]]
    w_re = jnp.concatenate(
        [seg(9), seg(10), seg(0), seg(1), seg(2), seg(3), seg(6), seg(7), seg(8), seg(4), seg(5),
         jnp.zeros((d, NP_COLS - COL_BA - 2 * DN_HEADS), w_in.dtype)], axis=1).astype(BF16)
    proj = _inproj(x2, mods, w_re)

    half = DA_HEAD_DIM // 2
    inv_freq = ROPE_THETA ** (-jnp.arange(half, dtype=F32) / half)
    invf = jnp.tile(inv_freq, LANES // half)[None, :]
    sgn = jnp.tile(jnp.concatenate([-jnp.ones((half,), F32), jnp.ones((half,), F32)]),
                   LANES // DA_HEAD_DIM)[None, :]
    qt4, k12, vt3 = _attnprep(positions.reshape(s, 1), invf, sgn, proj)
    lam_init = 0.8 - 0.6 * math.exp(-0.3 * layer)
    o_da = _attn(lam_init, qt4, k12, vt3, lambda_q1[None, :], lambda_k1[None, :],
                 lambda_q2[None, :], lambda_k2[None, :], da_norm_w[:, None])

    o_dn = _deltanet(proj, conv_w, _pad_lanes(dn_a_log, DN_HEADS), _pad_lanes(dn_dt_bias, DN_HEADS))

    wr_pad = jnp.zeros((d, LANES), F32).at[:, :N_EXPERTS].set(w_router)
    x1, u2, logits = _post(alpha, x2, o_dn, proj, o_da, mods, dn_norm_w[None, :],
                           w_dn_proj.astype(BF16), w_da_proj.astype(BF16), w_o.astype(BF16),
                           ln1_g[None, :], ln1_b[None, :], wr_pad, _pad_lanes(b_router, 0))

    idx_l, rank_l, w_top, counts_l = _route(logits)
    counts = counts_l[0, :N_EXPERTS]
    tm = MOE_TM
    nblk_e = (counts + tm - 1) // tm
    blk_end = jnp.cumsum(nblk_e)
    blk_start = blk_end - nblk_e
    pstart = (blk_start * tm).astype(I32)
    nb = (s * TOP_K) // tm + N_EXPERTS
    n_used = blk_end[-1]
    bids = jnp.arange(nb, dtype=I32)
    last_e = jnp.max(jnp.where(counts > 0, jnp.arange(N_EXPERTS, dtype=I32), 0))
    be = jnp.sum((bids[:, None] >= blk_end[None, :]).astype(I32), axis=1)
    be = jnp.where(bids < n_used, jnp.minimum(be, N_EXPERTS - 1), last_e).astype(I32)
    nvalid = jnp.clip(counts[be] - (bids - blk_start[be]) * tm, 0, tm)
    nvalid = jnp.where(bids < n_used, nvalid, 0).astype(I32)
    idx4 = idx_l[:, :TOP_K]
    e_ids = jnp.arange(N_EXPERTS, dtype=I32)
    dest = jnp.sum(jnp.where(idx4[:, :, None] == e_ids, pstart, 0), axis=-1) + rank_l[:, :TOP_K]
    dest = dest.astype(I32)

    n_used = n_used.astype(I32)[None]
    xs = _dispatch(pstart, counts.astype(I32), n_used,
                   dest.reshape(s // DISP_T, 1, DISP_T * TOP_K), u2, nb * tm)
    e, _, f = w_gate_up.shape
    bgu_perm = b_gate_up.reshape(e, f // PERM_W, LANES, 2).transpose(0, 1, 3, 2).reshape(e, 1, f)
    used = counts > 0
    expert_ord = (jnp.cumsum(used.astype(I32)) - 1)[be].astype(I32)
    later_used = used[None, :] & (e_ids[None, :] > e_ids[:, None])
    next_used = jnp.min(jnp.where(later_used, e_ids[None, :], N_EXPERTS), axis=1)
    next_used = jnp.where(next_used == N_EXPERTS, -1, next_used).astype(I32)
    ys = _experts(be, nvalid, n_used, expert_ord, next_used, xs, w_gate_up, bgu_perm, w_down,
                  b_down[:, None, :], _perm_matrix())
    return _combine(alpha, dest.reshape(s // COMB_T, 1, COMB_T * TOP_K), x1, w_top, mods,
                    ln2_g[None, :], ln2_b[None, :], ys)


def kernel(x, c, positions, w_ada, b_ada, w_in, conv_w, dn_a_log, dn_dt_bias, dn_norm_w, w_dn_proj, lambda_q1, lambda_k1, lambda_q2, lambda_k2, da_norm_w, w_da_proj, w_o, ln1_g, ln1_b, w_router, b_router, w_gate_up, b_gate_up, w_down, b_down, ln2_g, ln2_b):
    b, s, d = x.shape
    assert b == 1, "kernel is written for a single sequence"
    depth = w_ada.shape[0]
    alpha = (2.0 * depth) ** 0.25
    c8 = jnp.zeros((SUBLANES, d), F32).at[0:1, :].set(c.astype(F32))
    x2 = x.reshape(s, d)
    for l in range(depth):
        x2 = _layer(x2, c8, positions, l, w_ada[l], b_ada[l], w_in[l], conv_w[l], dn_a_log[l],
                    dn_dt_bias[l], dn_norm_w[l], w_dn_proj[l], lambda_q1[l], lambda_k1[l],
                    lambda_q2[l], lambda_k2[l], da_norm_w[l], w_da_proj[l], w_o[l], ln1_g[l],
                    ln1_b[l], w_router[l], b_router[l], w_gate_up[l], b_gate_up[l], w_down[l],
                    b_down[l], ln2_g[l], ln2_b[l], alpha)
    return x2.reshape(b, s, d)
```

```python
import functools
import math

import numpy as np
import jax
import jax.numpy as jnp
from jax import lax
from jax.experimental import pallas as pl
from jax.experimental.pallas import tpu as pltpu

F32 = jnp.float32
BF16 = jnp.bfloat16
I32 = jnp.int32
HIGHEST = lax.Precision.HIGHEST

DN_HEADS = 4
DN_HEAD_DIM = 128
DN_WIDTH = DN_HEADS * DN_HEAD_DIM
CONV_WIDTH = 4
DN_CHUNK = 64
DA_HEADS = 4
DA_HEAD_DIM = 64
DA_VDIM = 2 * DA_HEAD_DIM
DA_WIDTH = DA_HEADS * DA_VDIM
ROPE_THETA = 10000.0
N_EXPERTS = 32
TOP_K = 4
SWIGLU_LIMIT = 7.0
SWIGLU_ALPHA = 1.702
LN_EPS = 1e-5
L2_EPS = 1e-6

LANES = 128
SUBLANES = 8
VMEM_LIMIT = 56 * 1024 * 1024

COL_GDN, COL_GDA, COL_DQ, COL_DK, COL_DV, COL_Z, COL_AQ, COL_AK, COL_AV, COL_BA = (
    0, 1024, 2048, 2560, 3072, 3584, 4096, 4608, 5120, 5632)
NP_COLS = 5760

MOE_TM = 256


def _sigmoid(x):
    return 1.0 / (1.0 + jnp.exp(-x))


def _silu(x):
    return x * _sigmoid(x)


def _bdot(a, b):
    return jnp.dot(a.astype(BF16), b.astype(BF16), preferred_element_type=F32)


def _bdot_nt(a, b):
    return lax.dot_general(a.astype(BF16), b.astype(BF16), (((1,), (1,)), ((), ())),
                           preferred_element_type=F32)


def _bdot_tn(a, b):
    return lax.dot_general(a.astype(BF16), b.astype(BF16), (((0,), (0,)), ((), ())),
                           preferred_element_type=F32)


def _layer_norm_rows(r, g, b):
    mu = jnp.mean(r, axis=-1, keepdims=True)
    d = r - mu
    var = jnp.mean(d * d, axis=-1, keepdims=True)
    return d * lax.rsqrt(var + LN_EPS) * g + b


def _mods_body(c_ref, w_ref, b_ref, o_ref):
    c = c_ref[...]
    o_ref[...] = jnp.dot(_silu(c), w_ref[...], preferred_element_type=F32,
                         precision=HIGHEST) + b_ref[...]


def _mods(c8, w_ada, b_ada):
    d, n = w_ada.shape
    tn = 1536
    return pl.pallas_call(
        _mods_body,
        out_shape=jax.ShapeDtypeStruct((SUBLANES, n), F32),
        grid=(n // tn,),
        in_specs=[pl.BlockSpec((SUBLANES, d), lambda j: (0, 0)),
                  pl.BlockSpec((d, tn), lambda j: (0, j)),
                  pl.BlockSpec((1, tn), lambda j: (0, j))],
        out_specs=pl.BlockSpec((SUBLANES, tn), lambda j: (0, j)),
        compiler_params=pltpu.CompilerParams(dimension_semantics=("arbitrary",),
                                             vmem_limit_bytes=VMEM_LIMIT),
        name="mods",
    )(c8, w_ada, b_ada)


def _inproj_body(x_ref, shift_ref, scale_ref, w_ref, o_ref, u_ref):
    @pl.when(pl.program_id(1) == 0)
    def _():
        u = x_ref[...] * (1.0 + scale_ref[0:1, :]) + shift_ref[0:1, :]
        u_ref[...] = u.astype(BF16)

    o_ref[...] = jnp.dot(u_ref[...], w_ref[...], preferred_element_type=F32)


def _inproj(x2, mods, w_re):
    s, d = x2.shape
    tm = min(s, 1024)
    tn = 1152
    return pl.pallas_call(
        _inproj_body,
        out_shape=jax.ShapeDtypeStruct((s, NP_COLS), F32),
        grid=(s // tm, NP_COLS // tn),
        in_specs=[pl.BlockSpec((tm, d), lambda i, j: (i, 0)),
                  pl.BlockSpec((SUBLANES, d), lambda i, j: (0, 0)),
                  pl.BlockSpec((SUBLANES, d), lambda i, j: (0, 1)),
                  pl.BlockSpec((d, tn), lambda i, j: (0, j))],
        out_specs=pl.BlockSpec((tm, tn), lambda i, j: (i, j)),
        scratch_shapes=[pltpu.VMEM((tm, d), BF16)],
        compiler_params=pltpu.CompilerParams(dimension_semantics=("arbitrary", "arbitrary"),
                                             vmem_limit_bytes=VMEM_LIMIT),
        name="inproj",
    )(x2, mods, mods, w_re)


ATT_T = 512


def _attnprep_body(pos_ref, invf_ref, sgn_ref, aq_ref, ak_ref, av_ref, qt_ref, k_ref, vt_ref):
    ang = pos_ref[...].astype(F32) * invf_ref[...]
    cos = jnp.cos(ang)
    sin = jnp.sin(ang) * sgn_ref[...]
    t = ang.shape[0]
    lane = lax.broadcasted_iota(I32, (t, LANES), 1)
    first_half = (lane & (DA_HEAD_DIM - 1)) < (DA_HEAD_DIM // 2)
    low_rows = lax.broadcasted_iota(I32, (LANES, t), 0) < DA_HEAD_DIM

    def rope(x):
        swapped = jnp.where(first_half, pltpu.roll(x, LANES - DA_HEAD_DIM // 2, 1),
                            pltpu.roll(x, DA_HEAD_DIM // 2, 1))
        return x * cos + swapped * sin

    scale = DA_HEAD_DIM ** -0.5 * math.log2(math.e)
    for h in range(DA_HEADS):
        cols = slice(h * LANES, (h + 1) * LANES)
        qt = (rope(aq_ref[:, cols]) * scale).T
        qt_ref[h, 0, :, 0:t] = jnp.where(low_rows, qt, 0.0).astype(BF16)
        qt_ref[h, 0, :, t:2 * t] = jnp.where(low_rows, 0.0, qt).astype(BF16)
        k_ref[:, cols] = rope(ak_ref[:, cols]).astype(BF16)
        vt_ref[h, 0] = av_ref[:, cols].T.astype(BF16)


def _attnprep(pos_col, invf, sgn, proj):
    s = proj.shape[0]
    t = ATT_T
    nt = s // t
    w = DA_WIDTH
    return pl.pallas_call(
        _attnprep_body,
        out_shape=(jax.ShapeDtypeStruct((DA_HEADS, nt, LANES, 2 * t), BF16),
                   jax.ShapeDtypeStruct((s, w), BF16),
                   jax.ShapeDtypeStruct((DA_HEADS, nt, LANES, t), BF16)),
        grid=(nt,),
        in_specs=[pl.BlockSpec((t, 1), lambda i: (i, 0)),
                  pl.BlockSpec((1, LANES), lambda i: (0, 0)),
                  pl.BlockSpec((1, LANES), lambda i: (0, 0)),
                  pl.BlockSpec((t, w), lambda i: (i, COL_AQ // w)),
                  pl.BlockSpec((t, w), lambda i: (i, COL_AK // w)),
                  pl.BlockSpec((t, w), lambda i: (i, COL_AV // w))],
        out_specs=(pl.BlockSpec((DA_HEADS, 1, LANES, 2 * t), lambda i: (0, i, 0, 0)),
                   pl.BlockSpec((t, w), lambda i: (i, 0)),
                   pl.BlockSpec((DA_HEADS, 1, LANES, t), lambda i: (0, i, 0, 0))),
        compiler_params=pltpu.CompilerParams(dimension_semantics=("arbitrary",),
                                             vmem_limit_bytes=VMEM_LIMIT),
        name="attnprep",
    )(pos_col, invf, sgn, proj, proj, proj)


def _attn_body(lam_init, qt_ref, k_ref, vt_ref, lq1_ref, lk1_ref, lq2_ref, lk2_ref,
               nw_ref, o_ref, acc_ref, s0_ref, s1_ref):
    t = ATT_T
    i = pl.program_id(1)
    qt = qt_ref[...]

    def scores(j, dst):
        start = pl.multiple_of(j * t, t)
        dst[...] = jnp.dot(k_ref[pl.ds(start, t), :], qt, preferred_element_type=F32)

    def update(j, src, m_old, l_old, masked):
        st = src[...]
        if masked:
            row = lax.broadcasted_iota(I32, (t, 2 * t), 0)
            col = lax.broadcasted_iota(I32, (t, 2 * t), 1) & (t - 1)
            st = jnp.where(row <= col, st, -jnp.inf)
        m_new = jnp.maximum(m_old, jnp.max(st, axis=0, keepdims=True))
        alpha = jnp.exp2(m_old - m_new)
        p = jnp.exp2(st - m_new)
        l_new = alpha * l_old + jnp.sum(p, axis=0, keepdims=True)
        acc_ref[...] = alpha * acc_ref[...] + jnp.dot(vt_ref[j], p.astype(BF16),
                                                      preferred_element_type=F32)
        return m_new, l_new

    def finalize(l):
        lam = (jnp.exp(jnp.sum(lq1_ref[...] * lk1_ref[...], axis=-1, keepdims=True))
               - jnp.exp(jnp.sum(lq2_ref[...] * lk2_ref[...], axis=-1, keepdims=True)) + lam_init)
        ot = acc_ref[:, 0:t] / l[:, 0:t] - lam * (acc_ref[:, t:2 * t] / l[:, t:2 * t])
        ms = jnp.mean(ot * ot, axis=0, keepdims=True)
        ot = ot * lax.rsqrt(ms + LN_EPS) * nw_ref[...] * (1.0 - lam_init)
        o_ref[...] = ot.T

    scores(0, s0_ref)
    acc_ref[...] = jnp.zeros_like(acc_ref)

    def pair(j, carry):
        scores(j + 1, s1_ref)
        carry = update(j, s0_ref, *carry, False)
        scores(j + 2, s0_ref)
        return update(j + 1, s1_ref, *carry, False)

    def quad(p, carry):
        return pair(4 * p + 2, pair(4 * p, carry))

    n_quad = lax.shift_right_logical(i, 2)
    carry = lax.fori_loop(0, n_quad, quad,
                          (jnp.full((1, 2 * t), -jnp.inf, F32), jnp.zeros((1, 2 * t), F32)))
    n_pair = lax.shift_right_logical(i & 3, 1)
    m, l = lax.fori_loop(0, n_pair, lambda p, c: pair(4 * n_quad + 2 * p, c), carry)

    @pl.when((i & 1) == 0)
    def _():
        finalize(update(i, s0_ref, m, l, True)[1])

    @pl.when((i & 1) == 1)
    def _():
        scores(i, s1_ref)
        mid = update(i - 1, s0_ref, m, l, False)
        finalize(update(i, s1_ref, *mid, True)[1])


def _attn(lam_init, qt4, k12, vt3, lq1, lk1, lq2, lk2, nw_col):
    s = k12.shape[0]
    t = ATT_T
    nt = s // t
    small = pl.BlockSpec((1, DA_HEAD_DIM), lambda h, i: (0, 0))
    return pl.pallas_call(
        functools.partial(_attn_body, lam_init),
        out_shape=jax.ShapeDtypeStruct((s, DA_WIDTH), F32),
        grid=(DA_HEADS, nt),
        in_specs=[pl.BlockSpec((None, None, LANES, 2 * t), lambda h, i: (h, i, 0, 0)),
                  pl.BlockSpec((s, LANES), lambda h, i: (0, h)),
                  pl.BlockSpec((None, nt, LANES, t), lambda h, i: (h, 0, 0, 0)),
                  small, small, small, small,
                  pl.BlockSpec((DA_VDIM, 1), lambda h, i: (0, 0))],
        out_specs=pl.BlockSpec((t, LANES), lambda h, i: (i, h)),
        scratch_shapes=[pltpu.VMEM((LANES, 2 * t), F32),
                        pltpu.VMEM((t, 2 * t), F32), pltpu.VMEM((t, 2 * t), F32)],
        compiler_params=pltpu.CompilerParams(dimension_semantics=("arbitrary", "arbitrary"),
                                             vmem_limit_bytes=VMEM_LIMIT),
        name="attn",
    )(qt4, k12, vt3, lq1, lk1, lq2, lk2, nw_col)


DN_NC = 4
DN_TT = DN_NC * DN_CHUNK


def _tri_inverse(a_list, ii, jj, eye):
    zip_dot = lambda xs, ys: [_bdot(x, y) for x, y in zip(xs, ys)]
    blk8 = (ii >> 3) == (jj >> 3)
    n = [jnp.where(blk8, -a, 0.0) for a in a_list]
    n2 = zip_dot(n, n)
    n4 = zip_dot(n2, n2)
    nn2 = zip_dot(n, n2)
    p = [eye + x + y + z for x, y, z in zip(n, n2, nn2)]
    pn4 = zip_dot(p, n4)
    d = [x + y for x, y in zip(p, pn4)]
    for sh in (3, 4, 5):
        bi, bj = ii >> sh, jj >> sh
        below = ((bi & 1) == 1) & (bj == bi - 1)
        lo = [jnp.where(below, a, 0.0) for a in a_list]
        dld = zip_dot(zip_dot(d, lo), d)
        d = [x - y for x, y in zip(d, dld)]
    return d


def _cumsum_rows(x, row):
    shift = 1
    while shift < x.shape[0]:
        x = x + jnp.where(row >= shift, pltpu.roll(x, shift, 0), 0.0)
        shift *= 2
    return x


def _dn_body(q_ref, k_ref, v_ref, ba_ref, cw_ref, alog_ref, dtb_ref, o_ref, xbuf_ref, state_ref):
    tt = DN_TT
    c = DN_CHUNK
    nh = DN_HEADS
    dh = DN_HEAD_DIM
    r = nh * c

    @pl.when(pl.program_id(0) == 0)
    def _():
        xbuf_ref[:, 0:SUBLANES, :] = jnp.zeros((3, SUBLANES, DN_WIDTH), F32)
        state_ref[...] = jnp.zeros_like(state_ref)

    ys = []
    for n, ref in enumerate((q_ref, k_ref, v_ref)):
        cur = ref[...]
        xbuf_ref[n, SUBLANES:SUBLANES + tt, :] = cur
        acc = None
        for j in range(CONV_WIDTH):
            off = SUBLANES - (CONV_WIDTH - 1) + j
            term = cw_ref[j:j + 1, n * DN_WIDTH:(n + 1) * DN_WIDTH] * xbuf_ref[n, off:off + tt, :]
            acc = term if acc is None else acc + term
        xbuf_ref[n, 0:SUBLANES, :] = cur[tt - SUBLANES:tt, :]
        ys.append(_silu(acc))
    yq, yk, yv = ys

    ba = ba_ref[...]
    beta_full = _sigmoid(ba)
    g_full = -jnp.exp(alog_ref[...]) * jax.nn.softplus(ba + dtb_ref[...])

    qn, kn, vv = [], [], []
    for h in range(nh):
        cols = slice(h * dh, (h + 1) * dh)
        qh, kh = yq[:, cols], yk[:, cols]
        qn.append(qh * lax.rsqrt(jnp.sum(qh * qh, axis=-1, keepdims=True) + L2_EPS) * (dh ** -0.5))
        kn.append(kh * lax.rsqrt(jnp.sum(kh * kh, axis=-1, keepdims=True) + L2_EPS))
        vv.append(yv[:, cols])

    ii = lax.broadcasted_iota(I32, (r, r), 0)
    jj = lax.broadcasted_iota(I32, (r, r), 1)
    eye = (ii == jj).astype(F32)
    same_head = (ii >> 6) == (jj >> 6)
    incl = same_head & (ii >= jj)
    strict = same_head & (ii > jj)
    row_c = lax.broadcasted_iota(I32, (c, LANES), 0)
    chunks = range(DN_NC)

    def head_cols(x, lane0):
        return jnp.concatenate([x[:, lane0 + h:lane0 + h + 1] for h in range(nh)], axis=0)

    def head_rows(x, lane0):
        xt = jnp.concatenate([x, jnp.zeros_like(x)], axis=0).T
        return jnp.concatenate([xt[lane0 + h:lane0 + h + 1, 0:c] for h in range(nh)], axis=1)

    def stack(parts, rows):
        return jnp.concatenate([p[rows, :] for p in parts], axis=0)

    rows_of = [slice(ci * c, (ci + 1) * c) for ci in chunks]
    cg = [_cumsum_rows(g_full[rows, :], row_c) for rows in rows_of]
    tot = [x[c - 1:c, :] for x in cg]
    eg_st = [head_cols(jnp.exp(x), nh) for x in cg]
    erev_st = [head_cols(jnp.exp(t - x), nh) for x, t in zip(cg, tot)]
    etot = [jnp.exp(t) for t in tot]
    beta_st = [head_cols(beta_full[rows, :], 0) for rows in rows_of]
    decay = [jnp.exp(jnp.where(incl, head_cols(x, nh) - head_rows(x, nh), -jnp.inf)) for x in cg]
    k_st = [stack(kn, rows) for rows in rows_of]
    q_st = [stack(qn, rows) for rows in rows_of]
    v_st = [stack(vv, rows) for rows in rows_of]
    kk = [_bdot_nt(k, k) for k in k_st]
    qk = [_bdot_nt(q, k) * dc for q, k, dc in zip(q_st, k_st, decay)]
    a = [jnp.where(strict, b * dc * x, 0.0) for b, dc, x in zip(beta_st, decay, kk)]
    tinv = _tri_inverse(a, ii, jj, eye)
    rhs = [jnp.concatenate([b * v, (b * e) * k], axis=1)
           for b, e, k, v in zip(beta_st, eg_st, k_st, v_st)]
    uw = [_bdot(t, x) for t, x in zip(tinv, rhs)]
    qw = [_bdot(x, y) for x, y in zip(qk, uw)]
    o_loc = [x[:, 0:dh] for x in qw]
    q_eff = [q * e - x[:, dh:2 * dh] for q, e, x in zip(q_st, eg_st, qw)]
    k_dec = [k * e for k, e in zip(k_st, erev_st)]
    pb = [[_bdot_tn(kd[h * c:(h + 1) * c, :], x[h * c:(h + 1) * c, :]) for h in range(nh)]
          for kd, x in zip(k_dec, uw)]

    states = [state_ref[h] for h in range(nh)]
    for ci in chunks:
        for h in range(nh):
            hrows = slice(h * c, (h + 1) * c)
            lhs = jnp.concatenate([pb[ci][h][:, dh:2 * dh], q_eff[ci][hrows, :]], axis=0)
            ps = _bdot(lhs, states[h])
            gl = nh + h
            states[h] = states[h] * etot[ci][:, gl:gl + 1] - ps[0:dh, :] + pb[ci][h][:, 0:dh]
            o_ref[rows_of[ci], h * dh:(h + 1) * dh] = ps[dh:dh + c, :] + o_loc[ci][hrows, :]
    for h in range(nh):
        state_ref[h] = states[h]


def _deltanet(proj, conv_w, alog_row, dtb_row):
    s = proj.shape[0]
    tt = DN_TT
    w = DN_WIDTH
    return pl.pallas_call(
        _dn_body,
        out_shape=jax.ShapeDtypeStruct((s, w), F32),
        grid=(s // tt,),
        in_specs=[pl.BlockSpec((tt, w), lambda i: (i, COL_DQ // w)),
                  pl.BlockSpec((tt, w), lambda i: (i, COL_DK // w)),
                  pl.BlockSpec((tt, w), lambda i: (i, COL_DV // w)),
                  pl.BlockSpec((tt, LANES), lambda i: (i, COL_BA // LANES)),
                  pl.BlockSpec((CONV_WIDTH, 3 * w), lambda i: (0, 0)),
                  pl.BlockSpec((1, LANES), lambda i: (0, 0)),
                  pl.BlockSpec((1, LANES), lambda i: (0, 0))],
        out_specs=pl.BlockSpec((tt, w), lambda i: (i, 0)),
        scratch_shapes=[pltpu.VMEM((3, tt + SUBLANES, w), F32),
                        pltpu.VMEM((DN_HEADS, DN_HEAD_DIM, DN_HEAD_DIM), F32)],
        compiler_params=pltpu.CompilerParams(dimension_semantics=("arbitrary",),
                                             vmem_limit_bytes=VMEM_LIMIT),
        name="deltanet",
    )(proj, proj, proj, proj, conv_w, alog_row, dtb_row)


POST_SUB = 128

def _post_body(alpha, x_ref, odn_ref, z_ref, oda_ref, gdn_ref, gda_ref, gate_ref, scale_ref,
               shift_ref, dnw_ref, wdn_ref, wda_ref, wo_ref, lng_ref, lnb_ref, wr_ref, br_ref,
               x1_ref, u2_ref, lg_ref):
    for r0 in range(0, x_ref.shape[0], POST_SUB):
        rows = slice(r0, r0 + POST_SUB)
        odn = odn_ref[rows, :]
        z = z_ref[rows, :]
        parts = []
        for h in range(DN_HEADS):
            cols = slice(h * DN_HEAD_DIM, (h + 1) * DN_HEAD_DIM)
            o = odn[:, cols]
            o = o * lax.rsqrt(jnp.mean(o * o, axis=-1, keepdims=True) + LN_EPS) * dnw_ref[...]
            parts.append((o * _silu(z[:, cols])).astype(BF16))
        y_dn = jnp.dot(jnp.concatenate(parts, axis=-1), wdn_ref[...], preferred_element_type=F32)
        y_da = jnp.dot(oda_ref[rows, :].astype(BF16), wda_ref[...], preferred_element_type=F32)
        merged = _sigmoid(gdn_ref[rows, :]) * y_dn + _sigmoid(gda_ref[rows, :]) * y_da
        mix = jnp.dot(merged.astype(BF16), wo_ref[...], preferred_element_type=F32)
        r = alpha * x_ref[rows, :] + gate_ref[0:1, :] * mix
        x1 = _layer_norm_rows(r, lng_ref[...], lnb_ref[...])
        x1_ref[rows, :] = x1
        u2 = x1 * (1.0 + scale_ref[0:1, :]) + shift_ref[0:1, :]
        u2_ref[rows, :] = u2
        lg_ref[rows, :] = jnp.dot(u2.astype(BF16), wr_ref[...],
                                  preferred_element_type=F32) + br_ref[...]


def _post(alpha, x2, o_dn, proj, o_da, mods, dnw, wdn, wda, wo, lng, lnb, wr_pad, br_pad):
    s, d = x2.shape
    tm = min(s, 512)
    row = lambda i: (i, 0)
    const = lambda i: (0, 0)
    return pl.pallas_call(
        functools.partial(_post_body, alpha),
        out_shape=(jax.ShapeDtypeStruct((s, d), F32),
                   jax.ShapeDtypeStruct((s, d), F32),
                   jax.ShapeDtypeStruct((s, LANES), F32)),
        grid=(s // tm,),
        in_specs=[pl.BlockSpec((tm, d), row),
                  pl.BlockSpec((tm, DN_WIDTH), row),
                  pl.BlockSpec((tm, DN_WIDTH), lambda i: (i, COL_Z // DN_WIDTH)),
                  pl.BlockSpec((tm, DA_WIDTH), row),
                  pl.BlockSpec((tm, d), lambda i: (i, COL_GDN // d)),
                  pl.BlockSpec((tm, d), lambda i: (i, COL_GDA // d)),
                  pl.BlockSpec((SUBLANES, d), lambda i: (0, 2)),
                  pl.BlockSpec((SUBLANES, d), lambda i: (0, 4)),
                  pl.BlockSpec((SUBLANES, d), lambda i: (0, 3)),
                  pl.BlockSpec((1, DN_HEAD_DIM), const),
                  pl.BlockSpec((DN_WIDTH, d), const),
                  pl.BlockSpec((DA_WIDTH, d), const),
                  pl.BlockSpec((d, d), const),
                  pl.BlockSpec((1, d), const),
                  pl.BlockSpec((1, d), const),
                  pl.BlockSpec((d, LANES), const),
                  pl.BlockSpec((1, LANES), const)],
        out_specs=(pl.BlockSpec((tm, d), row), pl.BlockSpec((tm, d), row),
                   pl.BlockSpec((tm, LANES), row)),
        compiler_params=pltpu.CompilerParams(dimension_semantics=("arbitrary",),
                                             vmem_limit_bytes=VMEM_LIMIT),
        name="post",
    )(x2, o_dn, proj, o_da, proj, proj, mods, mods, mods, dnw, wdn, wda, wo, lng, lnb,
      wr_pad, br_pad)


ROUTE_T = 512


def _route_body(lg_ref, idx_ref, rank_ref, w_ref, cnt_ref, carry_ref):
    t = ROUTE_T

    @pl.when(pl.program_id(0) == 0)
    def _():
        carry_ref[...] = jnp.zeros_like(carry_ref)

    lane = lax.broadcasted_iota(I32, (t, LANES), 1)
    lane_f = lane.astype(F32)
    l = jnp.where(lane < N_EXPERTS, lg_ref[...], -jnp.inf)
    vals, idxs, sels = [], [], []
    for _ in range(TOP_K):
        mk = jnp.max(l, axis=-1, keepdims=True)
        ik = jnp.min(jnp.where(l == mk, lane_f, float(LANES)), axis=-1, keepdims=True)
        sel = lane_f == ik
        l = jnp.where(sel, -jnp.inf, l)
        vals.append(mk)
        idxs.append(ik)
        sels.append(sel)
    es = [jnp.exp(v - vals[0]) for v in vals]
    den = es[0] + es[1] + es[2] + es[3]
    onehot = jnp.zeros((t, LANES), F32)
    for sel in sels:
        onehot = jnp.where(sel, 1.0, onehot)
    ri = lax.broadcasted_iota(I32, (t, t), 0)
    ci = lax.broadcasted_iota(I32, (t, t), 1)
    before = (ci < ri).astype(BF16)
    rank_full = carry_ref[0:1, :] + jnp.dot(before, onehot.astype(BF16),
                                            preferred_element_type=F32)
    idx_out = jnp.zeros((t, LANES), F32)
    rank_out = jnp.zeros((t, LANES), F32)
    w_out = jnp.zeros((t, LANES), F32)
    for k in range(TOP_K):
        rk = jnp.sum(jnp.where(sels[k], rank_full, 0.0), axis=-1, keepdims=True)
        idx_out = jnp.where(lane == k, idxs[k], idx_out)
        rank_out = jnp.where(lane == k, rk, rank_out)
        w_out = jnp.where(lane == k, es[k] / den, w_out)
    idx_ref[...] = idx_out.astype(I32)
    rank_ref[...] = rank_out.astype(I32)
    w_ref[...] = w_out
    new = carry_ref[0:1, :] + jnp.sum(onehot, axis=0, keepdims=True)
    carry_ref[...] = jnp.broadcast_to(new, carry_ref.shape)
    cnt_ref[...] = jnp.broadcast_to(new, cnt_ref.shape).astype(I32)


def _route(logits):
    s = logits.shape[0]
    t = ROUTE_T
    row = pl.BlockSpec((t, LANES), lambda i: (i, 0))
    return pl.pallas_call(
        _route_body,
        out_shape=(jax.ShapeDtypeStruct((s, LANES), I32),
                   jax.ShapeDtypeStruct((s, LANES), I32),
                   jax.ShapeDtypeStruct((s, LANES), F32),
                   jax.ShapeDtypeStruct((SUBLANES, LANES), I32)),
        grid=(s // t,),
        in_specs=[row],
        out_specs=(row, row, row, pl.BlockSpec((SUBLANES, LANES), lambda i: (0, 0))),
        scratch_shapes=[pltpu.VMEM((SUBLANES, LANES), F32)],
        compiler_params=pltpu.CompilerParams(dimension_semantics=("arbitrary",),
                                             vmem_limit_bytes=VMEM_LIMIT),
        name="route",
    )(logits)


DISP_T = 512


def _dispatch_body(pstart_ref, cnt_ref, nused_ref, dest_ref, u_ref, xs_ref, zbuf_ref, sem, zsem):
    t = DISP_T
    tm = MOE_TM
    nb = xs_ref.shape[0] // tm

    @pl.when(pl.program_id(0) == 0)
    def _():
        zbuf_ref[...] = jnp.zeros_like(zbuf_ref)

        def pad_copy(e, r):
            return pltpu.make_async_copy(zbuf_ref.at[pl.ds(0, 1), :],
                                         xs_ref.at[pl.ds(pstart_ref[e] + cnt_ref[e] + r, 1), :], zsem)

        def blk_copy(b):
            return pltpu.make_async_copy(zbuf_ref, xs_ref.at[pl.ds(pl.multiple_of(b * tm, tm), tm), :],
                                         zsem)

        def per_expert(fn):
            def body(e, _):
                n_pad = (-cnt_ref[e]) & (tm - 1)
                lax.fori_loop(0, n_pad, lambda r, _: (fn(pad_copy(e, r)), 0)[1], 0)
                return 0
            lax.fori_loop(0, N_EXPERTS, body, 0)

        per_expert(lambda cp: cp.start())
        lax.fori_loop(nused_ref[0], nb, lambda b, _: (blk_copy(b).start(), 0)[1], 0)
        per_expert(lambda cp: cp.wait())
        lax.fori_loop(nused_ref[0], nb, lambda b, _: (blk_copy(b).wait(), 0)[1], 0)

    def copy(r, k):
        dest = dest_ref[0, 0, r * TOP_K + k]
        return pltpu.make_async_copy(u_ref.at[pl.ds(r, 1), :], xs_ref.at[pl.ds(dest, 1), :], sem)

    def start(r, _):
        for k in range(TOP_K):
            copy(r, k).start(priority=k % 2)
        return 0

    def wait(r, _):
        for k in range(TOP_K):
            copy(r, k).wait()
        return 0

    lax.fori_loop(0, t, start, 0, unroll=2)
    lax.fori_loop(0, t, wait, 0, unroll=8)


def _dispatch(pstart, counts, n_used, dest3, u2, n_rows):
    s, d = u2.shape
    t = DISP_T
    smem = lambda: pl.BlockSpec((1, 1, t * TOP_K), lambda i, *_: (i, 0, 0),
                                memory_space=pltpu.SMEM)
    return pl.pallas_call(
        _dispatch_body,
        out_shape=jax.ShapeDtypeStruct((n_rows, d), F32),
        grid_spec=pltpu.PrefetchScalarGridSpec(
            num_scalar_prefetch=3,
            grid=(s // t,),
            in_specs=[smem(), pl.BlockSpec((t, d), lambda i, *_: (i, 0))],
            out_specs=pl.BlockSpec(memory_space=pl.ANY),
            scratch_shapes=[pltpu.VMEM((MOE_TM, d), F32), pltpu.SemaphoreType.DMA,
                            pltpu.SemaphoreType.DMA]),
        compiler_params=pltpu.CompilerParams(dimension_semantics=("arbitrary",),
                                             vmem_limit_bytes=VMEM_LIMIT),
        name="dispatch",
    )(pstart, counts, n_used, dest3, u2)


PERM_W = 2 * LANES


def _experts_body(be_ref, nv_ref, nu_ref, eo_ref, nx_ref, x_ref, wgu_hbm, bgu_ref, wd_hbm, bd_ref,
                  perm_ref, o_ref, wgu_f, wd_f, wgu_s, wd_s, sems):
    b = pl.program_id(0)
    nvalid = nv_ref[b]
    e = be_ref[b]
    prev = be_ref[jnp.maximum(b - 1, 0)]
    changed = jnp.logical_or(b == 0, e != prev)
    f = wgu_s.shape[-1]

    def fetch(expert, slot):
        return (pltpu.make_async_copy(wgu_hbm.at[expert], wgu_f.at[slot], sems.at[slot]),
                pltpu.make_async_copy(wd_hbm.at[expert], wd_f.at[slot], sems.at[slot]))

    @pl.when(jnp.logical_and(changed, nvalid > 0))
    def _():
        slot = eo_ref[b] & 1

        @pl.when(b == 0)
        def _():
            for cp in fetch(e, slot):
                cp.start()

        for cp in fetch(e, slot):
            cp.wait()
        nxt = nx_ref[e]

        @pl.when(nxt >= 0)
        def _():
            for cp in fetch(nxt, 1 - slot):
                cp.start()

        for g in range(f // PERM_W):
            cols = slice(g * PERM_W, (g + 1) * PERM_W)
            wgu_s[:, cols] = jnp.dot(wgu_f[slot, :, cols].astype(BF16), perm_ref[...],
                                     preferred_element_type=F32).astype(BF16)
        wd_s[...] = wd_f[slot].astype(BF16)

    @pl.when(nvalid == 0)
    def _():
        o_ref[...] = jnp.zeros_like(o_ref)

    @pl.when(nvalid > 0)
    def _():
        x = x_ref[...].astype(BF16)
        gu = jnp.dot(x, wgu_s[...], preferred_element_type=F32) + bgu_ref[...]
        acts = []
        for g in range(f // PERM_W):
            glu = jnp.minimum(gu[:, g * PERM_W:g * PERM_W + LANES], SWIGLU_LIMIT)
            lin = jnp.clip(gu[:, g * PERM_W + LANES:(g + 1) * PERM_W], -SWIGLU_LIMIT, SWIGLU_LIMIT)
            acts.append((glu * _sigmoid(SWIGLU_ALPHA * glu) * (lin + 1.0)).astype(BF16))
        act = jnp.concatenate(acts, axis=-1)
        o_ref[...] = jnp.dot(act, wd_s[...], preferred_element_type=F32) + bd_ref[...]


def _experts(block_expert, block_nvalid, n_used, expert_ord, next_used, xs, w_gate_up, bgu_perm,
             w_down, b_down3, perm):
    n_rows, d = xs.shape
    tm = MOE_TM
    nb = n_rows // tm
    e, _, f = w_gate_up.shape
    de = w_down.shape[1]
    blk = lambda b, be, nv, nu, *_: (jnp.minimum(b, jnp.maximum(nu[0] - 1, 0)), 0)
    by_expert = lambda b, be, *_: (be[b], 0, 0)
    return pl.pallas_call(
        _experts_body,
        out_shape=jax.ShapeDtypeStruct((n_rows, d), F32),
        grid_spec=pltpu.PrefetchScalarGridSpec(
            num_scalar_prefetch=5,
            grid=(nb,),
            in_specs=[pl.BlockSpec((tm, d), blk),
                      pl.BlockSpec(memory_space=pl.ANY),
                      pl.BlockSpec((None, 1, f), by_expert),
                      pl.BlockSpec(memory_space=pl.ANY),
                      pl.BlockSpec((None, 1, d), by_expert),
                      pl.BlockSpec((PERM_W, PERM_W), lambda b, *_: (0, 0))],
            out_specs=pl.BlockSpec((tm, d), lambda b, *_: (b, 0)),
            scratch_shapes=[pltpu.VMEM((2, d, f), F32), pltpu.VMEM((2, de, d), F32),
                            pltpu.VMEM((d, f), BF16), pltpu.VMEM((de, d), BF16),
                            pltpu.SemaphoreType.DMA((2,))]),
        compiler_params=pltpu.CompilerParams(dimension_semantics=("arbitrary",),
                                             vmem_limit_bytes=VMEM_LIMIT),
        name="experts",
    )(block_expert, block_nvalid, n_used, expert_ord, next_used, xs, w_gate_up, bgu_perm, w_down,
      b_down3, perm)


COMB_T = 256


def _combine_body(alpha, cur_ref, nxt_ref, x1_ref, w_ref, gate_ref, lng_ref, lnb_ref,
                  ys_ref, o_ref, gbuf_ref, sems):
    t = COMB_T
    i = pl.program_id(0)
    slot = i & 1

    d = x1_ref.shape[1]

    def copy(src_ref, s, r, k):
        return pltpu.make_async_copy(ys_ref.at[pl.ds(src_ref[0, 0, r * TOP_K + k], 1), :],
                                     gbuf_ref.at[s, pl.ds(r, 1), pl.ds(k * d, d)], sems.at[s])

    def issue(src_ref, s):
        def body(r, _):
            for k in range(TOP_K):
                copy(src_ref, s, r, k).start(priority=k % 2)
            return 0
        lax.fori_loop(0, t, body, 0, unroll=2)

    @pl.when(i == 0)
    def _():
        issue(cur_ref, slot)

    @pl.when(i + 1 < pl.num_programs(0))
    def _():
        issue(nxt_ref, 1 - slot)

    def wait(r, _):
        for k in range(TOP_K):
            copy(cur_ref, slot, r, k).wait()
        return 0

    lax.fori_loop(0, t, wait, 0, unroll=8)
    w = w_ref[...]
    y = w[:, 0:1] * gbuf_ref[slot, :, 0:d]
    for k in range(1, TOP_K):
        y = y + w[:, k:k + 1] * gbuf_ref[slot, :, k * d:(k + 1) * d]
    r = alpha * x1_ref[...] + gate_ref[0:1, :] * y
    o_ref[...] = _layer_norm_rows(r, lng_ref[...], lnb_ref[...])


def _combine(alpha, dest3, x1, w_top, mods, lng, lnb, ys):
    s, d = x1.shape
    t = COMB_T
    n = s // t
    return pl.pallas_call(
        functools.partial(_combine_body, alpha),
        out_shape=jax.ShapeDtypeStruct((s, d), F32),
        grid=(n,),
        in_specs=[pl.BlockSpec((1, 1, t * TOP_K), lambda i: (i, 0, 0), memory_space=pltpu.SMEM),
                  pl.BlockSpec((1, 1, t * TOP_K), lambda i: (jnp.minimum(i + 1, n - 1), 0, 0),
                               memory_space=pltpu.SMEM),
                  pl.BlockSpec((t, d), lambda i: (i, 0)),
                  pl.BlockSpec((t, LANES), lambda i: (i, 0)),
                  pl.BlockSpec((SUBLANES, d), lambda i: (0, 5)),
                  pl.BlockSpec((1, d), lambda i: (0, 0)),
                  pl.BlockSpec((1, d), lambda i: (0, 0)),
                  pl.BlockSpec(memory_space=pl.ANY)],
        out_specs=pl.BlockSpec((t, d), lambda i: (i, 0)),
        scratch_shapes=[pltpu.VMEM((2, t, TOP_K * d), F32), pltpu.SemaphoreType.DMA((2,))],
        compiler_params=pltpu.CompilerParams(dimension_semantics=("arbitrary",),
                                             vmem_limit_bytes=VMEM_LIMIT),
        name="combine",
    )(dest3, dest3, x1, w_top, mods, lng, lnb, ys)


def _perm_matrix():
    p = np.zeros((PERM_W, PERM_W), np.float32)
    for c in range(PERM_W):
        p[c, (c // 2) + (LANES if c % 2 else 0)] = 1.0
    return jnp.asarray(p, BF16)


def _pad_lanes(v, offset):
    return jnp.zeros((1, LANES), F32).at[0, offset:offset + v.shape[0]].set(v.astype(F32))


def _layer(x2, silu_in, positions, layer, w_ada, b_ada, w_in, conv_w, dn_a_log, dn_dt_bias,
           dn_norm_w, w_dn_proj, lambda_q1, lambda_k1, lambda_q2, lambda_k2, da_norm_w,
           w_da_proj, w_o, ln1_g, ln1_b, w_router, b_router, w_gate_up, b_gate_up, w_down,
           b_down, ln2_g, ln2_b, alpha):
    s, d = x2.shape
    mods = _mods(silu_in, w_ada, b_ada[None, :])

    offs = np.cumsum([0, DN_WIDTH, DN_WIDTH, DN_WIDTH, DN_WIDTH, DN_HEADS, DN_HEADS,
                      DA_WIDTH, DA_WIDTH, DA_WIDTH, d, d])
    span = lambda a, b: w_in[:, offs[a]:offs[b]].astype(BF16)
    w_re = jnp.concatenate(
        [span(9, 11), span(0, 4), span(6, 9), span(4, 6),
         jnp.zeros((d, NP_COLS - COL_BA - 2 * DN_HEADS), BF16)], axis=1)
    proj = _inproj(x2, mods, w_re)

    half = DA_HEAD_DIM // 2
    inv_freq = ROPE_THETA ** (-jnp.arange(half, dtype=F32) / half)
    invf = jnp.tile(inv_freq, LANES // half)[None, :]
    sgn = jnp.tile(jnp.concatenate([-jnp.ones((half,), F32), jnp.ones((half,), F32)]),
                   LANES // DA_HEAD_DIM)[None, :]
    qt4, k12, vt3 = _attnprep(positions.reshape(s, 1), invf, sgn, proj)
    lam_init = 0.8 - 0.6 * math.exp(-0.3 * layer)
    o_da = _attn(lam_init, qt4, k12, vt3, lambda_q1[None, :], lambda_k1[None, :],
                 lambda_q2[None, :], lambda_k2[None, :], da_norm_w[:, None])

    o_dn = _deltanet(proj, conv_w, _pad_lanes(dn_a_log, DN_HEADS), _pad_lanes(dn_dt_bias, DN_HEADS))

    wr_pad = jnp.zeros((d, LANES), BF16).at[:, :N_EXPERTS].set(w_router.astype(BF16))
    x1, u2, logits = _post(alpha, x2, o_dn, proj, o_da, mods, dn_norm_w[None, :],
                           w_dn_proj.astype(BF16), w_da_proj.astype(BF16), w_o.astype(BF16),
                           ln1_g[None, :], ln1_b[None, :], wr_pad, _pad_lanes(b_router, 0))

    idx_l, rank_l, w_top, counts_l = _route(logits)
    counts = counts_l[0, :N_EXPERTS]
    tm = MOE_TM
    nblk_e = (counts + tm - 1) // tm
    blk_end = jnp.cumsum(nblk_e)
    blk_start = blk_end - nblk_e
    pstart = (blk_start * tm).astype(I32)
    nb = (s * TOP_K) // tm + N_EXPERTS
    n_used = blk_end[-1]
    bids = jnp.arange(nb, dtype=I32)
    last_e = jnp.max(jnp.where(counts > 0, jnp.arange(N_EXPERTS, dtype=I32), 0))
    be = jnp.sum((bids[:, None] >= blk_end[None, :]).astype(I32), axis=1)
    be = jnp.where(bids < n_used, jnp.minimum(be, N_EXPERTS - 1), last_e).astype(I32)
    nvalid = jnp.clip(counts[be] - (bids - blk_start[be]) * tm, 0, tm)
    nvalid = jnp.where(bids < n_used, nvalid, 0).astype(I32)
    e_ids = jnp.arange(N_EXPERTS, dtype=I32)
    idx_t = idx_l[:, :TOP_K].T
    dest_t = rank_l[:, :TOP_K].T
    for ex in range(N_EXPERTS):
        dest_t = dest_t + jnp.where(idx_t == ex, pstart[ex], 0)
    dest = dest_t.T.astype(I32)

    n_used = n_used.astype(I32)[None]
    xs = _dispatch(pstart, counts.astype(I32), n_used,
                   dest.reshape(s // DISP_T, 1, DISP_T * TOP_K), u2, nb * tm)
    e, _, f = w_gate_up.shape
    bgu_perm = b_gate_up.reshape(e, f // PERM_W, LANES, 2).transpose(0, 1, 3, 2).reshape(e, 1, f)
    used = counts > 0
    expert_ord = (jnp.cumsum(used.astype(I32)) - 1)[be].astype(I32)
    later_used = used[None, :] & (e_ids[None, :] > e_ids[:, None])
    next_used = jnp.min(jnp.where(later_used, e_ids[None, :], N_EXPERTS), axis=1)
    next_used = jnp.where(next_used == N_EXPERTS, -1, next_used).astype(I32)
    ys = _experts(be, nvalid, n_used, expert_ord, next_used, xs, w_gate_up, bgu_perm, w_down,
                  b_down[:, None, :], _perm_matrix())
    return _combine(alpha, dest.reshape(s // COMB_T, 1, COMB_T * TOP_K), x1, w_top, mods,
                    ln2_g[None, :], ln2_b[None, :], ys)


def kernel(x, c, positions, w_ada, b_ada, w_in, conv_w, dn_a_log, dn_dt_bias, dn_norm_w, w_dn_proj, lambda_q1, lambda_k1, lambda_q2, lambda_k2, da_norm_w, w_da_proj, w_o, ln1_g, ln1_b, w_router, b_router, w_gate_up, b_gate_up, w_down, b_down, ln2_g, ln2_b):
    b, s, d = x.shape
    assert b == 1, "kernel is written for a single sequence"
    depth = w_ada.shape[0]
    alpha = (2.0 * depth) ** 0.25
    c8 = jnp.zeros((SUBLANES, d), F32).at[0:1, :].set(c.astype(F32))
    x2 = x.reshape(s, d)
    for l in range(depth):
        x2 = _layer(x2, c8, positions, l, w_ada[l], b_ada[l], w_in[l], conv_w[l], dn_a_log[l],
                    dn_dt_bias[l], dn_norm_w[l], w_dn_proj[l], lambda_q1[l], lambda_k1[l],
                    lambda_q2[l], lambda_k2[l], da_norm_w[l], w_da_proj[l], w_o[l], ln1_g[l],
                    ln1_b[l], w_router[l], b_router[l], w_gate_up[l], b_gate_up[l], w_down[l],
                    b_down[l], ln2_g[l], ln2_b[l], alpha)
    return x2.reshape(b, s, d)
```

```python
import functools
import math

import numpy as np
import jax
import jax.numpy as jnp
from jax import lax
from jax.experimental import pallas as pl
from jax.experimental.pallas import tpu as pltpu

F32 = jnp.float32
BF16 = jnp.bfloat16
I32 = jnp.int32
HIGHEST = lax.Precision.HIGHEST

DN_HEADS = 4
DN_HEAD_DIM = 128
DN_WIDTH = DN_HEADS * DN_HEAD_DIM
CONV_WIDTH = 4
DN_CHUNK = 64
DA_HEADS = 4
DA_HEAD_DIM = 64
DA_VDIM = 2 * DA_HEAD_DIM
DA_WIDTH = DA_HEADS * DA_VDIM
ROPE_THETA = 10000.0
N_EXPERTS = 32
TOP_K = 4
SWIGLU_LIMIT = 7.0
SWIGLU_ALPHA = 1.702
LN_EPS = 1e-5
L2_EPS = 1e-6

LANES = 128
SUBLANES = 8
VMEM_LIMIT = 56 * 1024 * 1024

COL_GDN, COL_GDA, COL_DQ, COL_DK, COL_DV, COL_Z, COL_AQ, COL_AK, COL_AV, COL_BA = (
    0, 1024, 2048, 2560, 3072, 3584, 4096, 4608, 5120, 5632)
NP_COLS = 5760

MOE_TM = 256


def _sigmoid(x):
    return 1.0 / (1.0 + jnp.exp(-x))


def _silu(x):
    return x * _sigmoid(x)


def _bdot(a, b):
    return jnp.dot(a.astype(BF16), b.astype(BF16), preferred_element_type=F32)


def _bdot_nt(a, b):
    return lax.dot_general(a.astype(BF16), b.astype(BF16), (((1,), (1,)), ((), ())),
                           preferred_element_type=F32)


def _bdot_tn(a, b):
    return lax.dot_general(a.astype(BF16), b.astype(BF16), (((0,), (0,)), ((), ())),
                           preferred_element_type=F32)


def _layer_norm_rows(r, g, b):
    mu = jnp.mean(r, axis=-1, keepdims=True)
    d = r - mu
    var = jnp.mean(d * d, axis=-1, keepdims=True)
    return d * lax.rsqrt(var + LN_EPS) * g + b


def _mods_body(c_ref, w_ref, b_ref, o_ref):
    c = c_ref[...]
    o_ref[...] = jnp.dot(_silu(c), w_ref[...], preferred_element_type=F32,
                         precision=HIGHEST) + b_ref[...]


def _mods(c8, w_ada, b_ada):
    d, n = w_ada.shape
    tn = 1536
    return pl.pallas_call(
        _mods_body,
        out_shape=jax.ShapeDtypeStruct((SUBLANES, n), F32),
        grid=(n // tn,),
        in_specs=[pl.BlockSpec((SUBLANES, d), lambda j: (0, 0)),
                  pl.BlockSpec((d, tn), lambda j: (0, j)),
                  pl.BlockSpec((1, tn), lambda j: (0, j))],
        out_specs=pl.BlockSpec((SUBLANES, tn), lambda j: (0, j)),
        compiler_params=pltpu.CompilerParams(dimension_semantics=("arbitrary",),
                                             vmem_limit_bytes=VMEM_LIMIT),
        name="mods",
    )(c8, w_ada, b_ada)


def _inproj_body(x_ref, shift_ref, scale_ref, w_ref, o_ref, u_ref):
    @pl.when(pl.program_id(1) == 0)
    def _():
        u = x_ref[...] * (1.0 + scale_ref[0:1, :]) + shift_ref[0:1, :]
        u_ref[...] = u.astype(BF16)

    o_ref[...] = jnp.dot(u_ref[...], w_ref[...], preferred_element_type=F32)


def _inproj(x2, mods, w_re):
    s, d = x2.shape
    tm = min(s, 1024)
    tn = 1152
    return pl.pallas_call(
        _inproj_body,
        out_shape=jax.ShapeDtypeStruct((s, NP_COLS), F32),
        grid=(s // tm, NP_COLS // tn),
        in_specs=[pl.BlockSpec((tm, d), lambda i, j: (i, 0)),
                  pl.BlockSpec((SUBLANES, d), lambda i, j: (0, 0)),
                  pl.BlockSpec((SUBLANES, d), lambda i, j: (0, 1)),
                  pl.BlockSpec((d, tn), lambda i, j: (0, j))],
        out_specs=pl.BlockSpec((tm, tn), lambda i, j: (i, j)),
        scratch_shapes=[pltpu.VMEM((tm, d), BF16)],
        compiler_params=pltpu.CompilerParams(dimension_semantics=("arbitrary", "arbitrary"),
                                             vmem_limit_bytes=VMEM_LIMIT),
        name="inproj",
    )(x2, mods, mods, w_re)


ATT_T = 512


def _attnprep_body(pos_ref, invf_ref, sgn_ref, aq_ref, ak_ref, av_ref, qt_ref, k_ref, vt_ref):
    ang = pos_ref[...].astype(F32) * invf_ref[...]
    cos = jnp.cos(ang)
    sin = jnp.sin(ang) * sgn_ref[...]
    t = ang.shape[0]
    lane = lax.broadcasted_iota(I32, (t, LANES), 1)
    first_half = (lane & (DA_HEAD_DIM - 1)) < (DA_HEAD_DIM // 2)
    low_rows = lax.broadcasted_iota(I32, (LANES, t), 0) < DA_HEAD_DIM

    def rope(x):
        swapped = jnp.where(first_half, pltpu.roll(x, LANES - DA_HEAD_DIM // 2, 1),
                            pltpu.roll(x, DA_HEAD_DIM // 2, 1))
        return x * cos + swapped * sin

    scale = DA_HEAD_DIM ** -0.5 * math.log2(math.e)
    for h in range(DA_HEADS):
        cols = slice(h * LANES, (h + 1) * LANES)
        qt = (rope(aq_ref[:, cols]) * scale).T
        qt_ref[h, 0, :, 0:t] = jnp.where(low_rows, qt, 0.0).astype(BF16)
        qt_ref[h, 0, :, t:2 * t] = jnp.where(low_rows, 0.0, qt).astype(BF16)
        k_ref[:, cols] = rope(ak_ref[:, cols]).astype(BF16)
        vt_ref[h, 0] = av_ref[:, cols].T.astype(BF16)


def _attnprep(pos_col, invf, sgn, proj):
    s = proj.shape[0]
    t = ATT_T
    nt = s // t
    w = DA_WIDTH
    return pl.pallas_call(
        _attnprep_body,
        out_shape=(jax.ShapeDtypeStruct((DA_HEADS, nt, LANES, 2 * t), BF16),
                   jax.ShapeDtypeStruct((s, w), BF16),
                   jax.ShapeDtypeStruct((DA_HEADS, nt, LANES, t), BF16)),
        grid=(nt,),
        in_specs=[pl.BlockSpec((t, 1), lambda i: (i, 0)),
                  pl.BlockSpec((1, LANES), lambda i: (0, 0)),
                  pl.BlockSpec((1, LANES), lambda i: (0, 0)),
                  pl.BlockSpec((t, w), lambda i: (i, COL_AQ // w)),
                  pl.BlockSpec((t, w), lambda i: (i, COL_AK // w)),
                  pl.BlockSpec((t, w), lambda i: (i, COL_AV // w))],
        out_specs=(pl.BlockSpec((DA_HEADS, 1, LANES, 2 * t), lambda i: (0, i, 0, 0)),
                   pl.BlockSpec((t, w), lambda i: (i, 0)),
                   pl.BlockSpec((DA_HEADS, 1, LANES, t), lambda i: (0, i, 0, 0))),
        compiler_params=pltpu.CompilerParams(dimension_semantics=("arbitrary",),
                                             vmem_limit_bytes=VMEM_LIMIT),
        name="attnprep",
    )(pos_col, invf, sgn, proj, proj, proj)


def _attn_body(lam_init, qt_ref, k_ref, vt_ref, lq1_ref, lk1_ref, lq2_ref, lk2_ref,
               nw_ref, o_ref, acc_ref, s0_ref, s1_ref):
    t = ATT_T
    i = pl.program_id(1)
    qt = qt_ref[...]

    def scores(j, dst):
        start = pl.multiple_of(j * t, t)
        dst[...] = jnp.dot(k_ref[pl.ds(start, t), :], qt, preferred_element_type=F32)

    def update(j, src, m_old, l_old, masked):
        st = src[...]
        if masked:
            row = lax.broadcasted_iota(I32, (t, 2 * t), 0)
            col = lax.broadcasted_iota(I32, (t, 2 * t), 1) & (t - 1)
            st = jnp.where(row <= col, st, -jnp.inf)
        m_new = jnp.maximum(m_old, jnp.max(st, axis=0, keepdims=True))
        alpha = jnp.exp2(m_old - m_new)
        p = jnp.exp2(st - m_new)
        l_new = alpha * l_old + jnp.sum(p, axis=0, keepdims=True)
        acc_ref[...] = alpha * acc_ref[...] + jnp.dot(vt_ref[j], p.astype(BF16),
                                                      preferred_element_type=F32)
        return m_new, l_new

    def finalize(l):
        lam = (jnp.exp(jnp.sum(lq1_ref[...] * lk1_ref[...], axis=-1, keepdims=True))
               - jnp.exp(jnp.sum(lq2_ref[...] * lk2_ref[...], axis=-1, keepdims=True)) + lam_init)
        ot = acc_ref[:, 0:t] / l[:, 0:t] - lam * (acc_ref[:, t:2 * t] / l[:, t:2 * t])
        ms = jnp.mean(ot * ot, axis=0, keepdims=True)
        ot = ot * lax.rsqrt(ms + LN_EPS) * nw_ref[...] * (1.0 - lam_init)
        o_ref[...] = ot.T

    scores(0, s0_ref)
    acc_ref[...] = jnp.zeros_like(acc_ref)

    def pair(j, carry):
        scores(j + 1, s1_ref)
        carry = update(j, s0_ref, *carry, False)
        scores(j + 2, s0_ref)
        return update(j + 1, s1_ref, *carry, False)

    def quad(p, carry):
        return pair(4 * p + 2, pair(4 * p, carry))

    n_quad = lax.shift_right_logical(i, 2)
    carry = lax.fori_loop(0, n_quad, quad,
                          (jnp.full((1, 2 * t), -jnp.inf, F32), jnp.zeros((1, 2 * t), F32)))
    n_pair = lax.shift_right_logical(i & 3, 1)
    m, l = lax.fori_loop(0, n_pair, lambda p, c: pair(4 * n_quad + 2 * p, c), carry)

    @pl.when((i & 1) == 0)
    def _():
        finalize(update(i, s0_ref, m, l, True)[1])

    @pl.when((i & 1) == 1)
    def _():
        scores(i, s1_ref)
        mid = update(i - 1, s0_ref, m, l, False)
        finalize(update(i, s1_ref, *mid, True)[1])


def _attn(lam_init, qt4, k12, vt3, lq1, lk1, lq2, lk2, nw_col):
    s = k12.shape[0]
    t = ATT_T
    nt = s // t
    small = pl.BlockSpec((1, DA_HEAD_DIM), lambda h, i: (0, 0))
    return pl.pallas_call(
        functools.partial(_attn_body, lam_init),
        out_shape=jax.ShapeDtypeStruct((s, DA_WIDTH), F32),
        grid=(DA_HEADS, nt),
        in_specs=[pl.BlockSpec((None, None, LANES, 2 * t), lambda h, i: (h, i, 0, 0)),
                  pl.BlockSpec((s, LANES), lambda h, i: (0, h)),
                  pl.BlockSpec((None, nt, LANES, t), lambda h, i: (h, 0, 0, 0)),
                  small, small, small, small,
                  pl.BlockSpec((DA_VDIM, 1), lambda h, i: (0, 0))],
        out_specs=pl.BlockSpec((t, LANES), lambda h, i: (i, h)),
        scratch_shapes=[pltpu.VMEM((LANES, 2 * t), F32),
                        pltpu.VMEM((t, 2 * t), F32), pltpu.VMEM((t, 2 * t), F32)],
        compiler_params=pltpu.CompilerParams(dimension_semantics=("arbitrary", "arbitrary"),
                                             vmem_limit_bytes=VMEM_LIMIT),
        name="attn",
    )(qt4, k12, vt3, lq1, lk1, lq2, lk2, nw_col)


DN_NC = 4
DN_TT = DN_NC * DN_CHUNK


def _tri_inverse(a_list, ii, jj, eye):
    zip_dot = lambda xs, ys: [_bdot(x, y) for x, y in zip(xs, ys)]
    blk8 = (ii >> 3) == (jj >> 3)
    n = [jnp.where(blk8, -a, 0.0) for a in a_list]
    n2 = zip_dot(n, n)
    n4 = zip_dot(n2, n2)
    nn2 = zip_dot(n, n2)
    p = [eye + x + y + z for x, y, z in zip(n, n2, nn2)]
    pn4 = zip_dot(p, n4)
    d = [x + y for x, y in zip(p, pn4)]
    for sh in (3, 4, 5):
        bi, bj = ii >> sh, jj >> sh
        below = ((bi & 1) == 1) & (bj == bi - 1)
        lo = [jnp.where(below, a, 0.0) for a in a_list]
        dld = zip_dot(zip_dot(d, lo), d)
        d = [x - y for x, y in zip(d, dld)]
    return d


def _cumsum_rows(x, row):
    shift = 1
    while shift < x.shape[0]:
        x = x + jnp.where(row >= shift, pltpu.roll(x, shift, 0), 0.0)
        shift *= 2
    return x


def _dn_body(q_ref, k_ref, v_ref, ba_ref, cw_ref, alog_ref, dtb_ref, o_ref, xbuf_ref, state_ref):
    tt = DN_TT
    c = DN_CHUNK
    nh = DN_HEADS
    dh = DN_HEAD_DIM
    r = nh * c

    @pl.when(pl.program_id(0) == 0)
    def _():
        xbuf_ref[:, 0:SUBLANES, :] = jnp.zeros((3, SUBLANES, DN_WIDTH), F32)
        state_ref[...] = jnp.zeros_like(state_ref)

    ys = []
    for n, ref in enumerate((q_ref, k_ref, v_ref)):
        cur = ref[...]
        xbuf_ref[n, SUBLANES:SUBLANES + tt, :] = cur
        acc = None
        for j in range(CONV_WIDTH):
            off = SUBLANES - (CONV_WIDTH - 1) + j
            term = cw_ref[j:j + 1, n * DN_WIDTH:(n + 1) * DN_WIDTH] * xbuf_ref[n, off:off + tt, :]
            acc = term if acc is None else acc + term
        xbuf_ref[n, 0:SUBLANES, :] = cur[tt - SUBLANES:tt, :]
        ys.append(_silu(acc))
    yq, yk, yv = ys

    ba = ba_ref[...]
    beta_full = _sigmoid(ba)
    g_full = -jnp.exp(alog_ref[...]) * jax.nn.softplus(ba + dtb_ref[...])

    qn, kn, vv = [], [], []
    for h in range(nh):
        cols = slice(h * dh, (h + 1) * dh)
        qh, kh = yq[:, cols], yk[:, cols]
        qn.append(qh * lax.rsqrt(jnp.sum(qh * qh, axis=-1, keepdims=True) + L2_EPS) * (dh ** -0.5))
        kn.append(kh * lax.rsqrt(jnp.sum(kh * kh, axis=-1, keepdims=True) + L2_EPS))
        vv.append(yv[:, cols])

    ii = lax.broadcasted_iota(I32, (r, r), 0)
    jj = lax.broadcasted_iota(I32, (r, r), 1)
    eye = (ii == jj).astype(F32)
    same_head = (ii >> 6) == (jj >> 6)
    incl = same_head & (ii >= jj)
    strict = same_head & (ii > jj)
    row_c = lax.broadcasted_iota(I32, (c, LANES), 0)
    chunks = range(DN_NC)

    def head_cols(x, lane0):
        return jnp.concatenate([x[:, lane0 + h:lane0 + h + 1] for h in range(nh)], axis=0)

    def head_rows(x, lane0):
        xt = jnp.concatenate([x, jnp.zeros_like(x)], axis=0).T
        return jnp.concatenate([xt[lane0 + h:lane0 + h + 1, 0:c] for h in range(nh)], axis=1)

    def stack(parts, rows):
        return jnp.concatenate([p[rows, :] for p in parts], axis=0)

    rows_of = [slice(ci * c, (ci + 1) * c) for ci in chunks]
    cg = [_cumsum_rows(g_full[rows, :], row_c) for rows in rows_of]
    tot = [x[c - 1:c, :] for x in cg]
    eg_st = [head_cols(jnp.exp(x), nh) for x in cg]
    erev_st = [head_cols(jnp.exp(t - x), nh) for x, t in zip(cg, tot)]
    etot = [jnp.exp(t) for t in tot]
    beta_st = [head_cols(beta_full[rows, :], 0) for rows in rows_of]
    decay = [jnp.exp(jnp.where(incl, head_cols(x, nh) - head_rows(x, nh), -jnp.inf)) for x in cg]
    k_st = [stack(kn, rows) for rows in rows_of]
    q_st = [stack(qn, rows) for rows in rows_of]
    v_st = [stack(vv, rows) for rows in rows_of]
    kk = [_bdot_nt(k, k) for k in k_st]
    qk = [_bdot_nt(q, k) * dc for q, k, dc in zip(q_st, k_st, decay)]
    a = [jnp.where(strict, b * dc * x, 0.0) for b, dc, x in zip(beta_st, decay, kk)]
    tinv = _tri_inverse(a, ii, jj, eye)
    rhs = [jnp.concatenate([b * v, (b * e) * k], axis=1)
           for b, e, k, v in zip(beta_st, eg_st, k_st, v_st)]
    uw = [_bdot(t, x) for t, x in zip(tinv, rhs)]
    qw = [_bdot(x, y) for x, y in zip(qk, uw)]
    o_loc = [x[:, 0:dh] for x in qw]
    q_eff = [q * e - x[:, dh:2 * dh] for q, e, x in zip(q_st, eg_st, qw)]
    k_dec = [k * e for k, e in zip(k_st, erev_st)]
    pb = [[_bdot_tn(kd[h * c:(h + 1) * c, :], x[h * c:(h + 1) * c, :]) for h in range(nh)]
          for kd, x in zip(k_dec, uw)]

    states = [state_ref[h] for h in range(nh)]
    for ci in chunks:
        for h in range(nh):
            hrows = slice(h * c, (h + 1) * c)
            lhs = jnp.concatenate([pb[ci][h][:, dh:2 * dh], q_eff[ci][hrows, :]], axis=0)
            ps = _bdot(lhs, states[h])
            gl = nh + h
            states[h] = states[h] * etot[ci][:, gl:gl + 1] - ps[0:dh, :] + pb[ci][h][:, 0:dh]
            o_ref[rows_of[ci], h * dh:(h + 1) * dh] = ps[dh:dh + c, :] + o_loc[ci][hrows, :]
    for h in range(nh):
        state_ref[h] = states[h]


def _deltanet(proj, conv_w, alog_row, dtb_row):
    s = proj.shape[0]
    tt = DN_TT
    w = DN_WIDTH
    return pl.pallas_call(
        _dn_body,
        out_shape=jax.ShapeDtypeStruct((s, w), F32),
        grid=(s // tt,),
        in_specs=[pl.BlockSpec((tt, w), lambda i: (i, COL_DQ // w)),
                  pl.BlockSpec((tt, w), lambda i: (i, COL_DK // w)),
                  pl.BlockSpec((tt, w), lambda i: (i, COL_DV // w)),
                  pl.BlockSpec((tt, LANES), lambda i: (i, COL_BA // LANES)),
                  pl.BlockSpec((CONV_WIDTH, 3 * w), lambda i: (0, 0)),
                  pl.BlockSpec((1, LANES), lambda i: (0, 0)),
                  pl.BlockSpec((1, LANES), lambda i: (0, 0))],
        out_specs=pl.BlockSpec((tt, w), lambda i: (i, 0)),
        scratch_shapes=[pltpu.VMEM((3, tt + SUBLANES, w), F32),
                        pltpu.VMEM((DN_HEADS, DN_HEAD_DIM, DN_HEAD_DIM), F32)],
        compiler_params=pltpu.CompilerParams(dimension_semantics=("arbitrary",),
                                             vmem_limit_bytes=VMEM_LIMIT),
        name="deltanet",
    )(proj, proj, proj, proj, conv_w, alog_row, dtb_row)


POST_SUB = 128

def _post_body(alpha, x_ref, odn_ref, z_ref, oda_ref, gdn_ref, gda_ref, gate_ref, scale_ref,
               shift_ref, dnw_ref, wdn_ref, wda_ref, wo_ref, lng_ref, lnb_ref, wr_ref, br_ref,
               x1_ref, u2_ref, lg_ref):
    for r0 in range(0, x_ref.shape[0], POST_SUB):
        rows = slice(r0, r0 + POST_SUB)
        odn = odn_ref[rows, :]
        z = z_ref[rows, :]
        parts = []
        for h in range(DN_HEADS):
            cols = slice(h * DN_HEAD_DIM, (h + 1) * DN_HEAD_DIM)
            o = odn[:, cols]
            o = o * lax.rsqrt(jnp.mean(o * o, axis=-1, keepdims=True) + LN_EPS) * dnw_ref[...]
            parts.append((o * _silu(z[:, cols])).astype(BF16))
        y_dn = jnp.dot(jnp.concatenate(parts, axis=-1), wdn_ref[...], preferred_element_type=F32)
        y_da = jnp.dot(oda_ref[rows, :].astype(BF16), wda_ref[...], preferred_element_type=F32)
        merged = _sigmoid(gdn_ref[rows, :]) * y_dn + _sigmoid(gda_ref[rows, :]) * y_da
        mix = jnp.dot(merged.astype(BF16), wo_ref[...], preferred_element_type=F32)
        r = alpha * x_ref[rows, :] + gate_ref[0:1, :] * mix
        x1 = _layer_norm_rows(r, lng_ref[...], lnb_ref[...])
        x1_ref[rows, :] = x1
        u2 = x1 * (1.0 + scale_ref[0:1, :]) + shift_ref[0:1, :]
        u2_ref[rows, :] = u2
        lg_ref[rows, :] = jnp.dot(u2.astype(BF16), wr_ref[...],
                                  preferred_element_type=F32) + br_ref[...]


def _post(alpha, x2, o_dn, proj, o_da, mods, dnw, wdn, wda, wo, lng, lnb, wr_pad, br_pad):
    s, d = x2.shape
    tm = min(s, 512)
    row = lambda i: (i, 0)
    const = lambda i: (0, 0)
    return pl.pallas_call(
        functools.partial(_post_body, alpha),
        out_shape=(jax.ShapeDtypeStruct((s, d), F32),
                   jax.ShapeDtypeStruct((s, d), F32),
                   jax.ShapeDtypeStruct((s, LANES), F32)),
        grid=(s // tm,),
        in_specs=[pl.BlockSpec((tm, d), row),
                  pl.BlockSpec((tm, DN_WIDTH), row),
                  pl.BlockSpec((tm, DN_WIDTH), lambda i: (i, COL_Z // DN_WIDTH)),
                  pl.BlockSpec((tm, DA_WIDTH), row),
                  pl.BlockSpec((tm, d), lambda i: (i, COL_GDN // d)),
                  pl.BlockSpec((tm, d), lambda i: (i, COL_GDA // d)),
                  pl.BlockSpec((SUBLANES, d), lambda i: (0, 2)),
                  pl.BlockSpec((SUBLANES, d), lambda i: (0, 4)),
                  pl.BlockSpec((SUBLANES, d), lambda i: (0, 3)),
                  pl.BlockSpec((1, DN_HEAD_DIM), const),
                  pl.BlockSpec((DN_WIDTH, d), const),
                  pl.BlockSpec((DA_WIDTH, d), const),
                  pl.BlockSpec((d, d), const),
                  pl.BlockSpec((1, d), const),
                  pl.BlockSpec((1, d), const),
                  pl.BlockSpec((d, LANES), const),
                  pl.BlockSpec((1, LANES), const)],
        out_specs=(pl.BlockSpec((tm, d), row), pl.BlockSpec((tm, d), row),
                   pl.BlockSpec((tm, LANES), row)),
        compiler_params=pltpu.CompilerParams(dimension_semantics=("arbitrary",),
                                             vmem_limit_bytes=VMEM_LIMIT),
        name="post",
    )(x2, o_dn, proj, o_da, proj, proj, mods, mods, mods, dnw, wdn, wda, wo, lng, lnb,
      wr_pad, br_pad)


ROUTE_T = 512


def _route_body(lg_ref, idx_ref, rank_ref, w_ref, cnt_ref, carry_ref):
    t = ROUTE_T

    @pl.when(pl.program_id(0) == 0)
    def _():
        carry_ref[...] = jnp.zeros_like(carry_ref)

    lane = lax.broadcasted_iota(I32, (t, LANES), 1)
    lane_f = lane.astype(F32)
    l = jnp.where(lane < N_EXPERTS, lg_ref[...], -jnp.inf)
    vals, idxs, sels = [], [], []
    for _ in range(TOP_K):
        mk = jnp.max(l, axis=-1, keepdims=True)
        ik = jnp.min(jnp.where(l == mk, lane_f, float(LANES)), axis=-1, keepdims=True)
        sel = lane_f == ik
        l = jnp.where(sel, -jnp.inf, l)
        vals.append(mk)
        idxs.append(ik)
        sels.append(sel)
    es = [jnp.exp(v - vals[0]) for v in vals]
    den = es[0] + es[1] + es[2] + es[3]
    onehot = jnp.zeros((t, LANES), F32)
    for sel in sels:
        onehot = jnp.where(sel, 1.0, onehot)
    ri = lax.broadcasted_iota(I32, (t, t), 0)
    ci = lax.broadcasted_iota(I32, (t, t), 1)
    before = (ci < ri).astype(BF16)
    rank_full = carry_ref[0:1, :] + jnp.dot(before, onehot.astype(BF16),
                                            preferred_element_type=F32)
    idx_out = jnp.zeros((t, LANES), F32)
    rank_out = jnp.zeros((t, LANES), F32)
    w_out = jnp.zeros((t, LANES), F32)
    for k in range(TOP_K):
        rk = jnp.sum(jnp.where(sels[k], rank_full, 0.0), axis=-1, keepdims=True)
        idx_out = jnp.where(lane == k, idxs[k], idx_out)
        rank_out = jnp.where(lane == k, rk, rank_out)
        w_out = jnp.where(lane == k, es[k] / den, w_out)
    idx_ref[...] = idx_out.astype(I32)
    rank_ref[...] = rank_out.astype(I32)
    w_ref[...] = w_out
    new = carry_ref[0:1, :] + jnp.sum(onehot, axis=0, keepdims=True)
    carry_ref[...] = jnp.broadcast_to(new, carry_ref.shape)
    cnt_ref[...] = jnp.broadcast_to(new, cnt_ref.shape).astype(I32)


def _route(logits):
    s = logits.shape[0]
    t = ROUTE_T
    row = pl.BlockSpec((t, LANES), lambda i: (i, 0))
    return pl.pallas_call(
        _route_body,
        out_shape=(jax.ShapeDtypeStruct((s, LANES), I32),
                   jax.ShapeDtypeStruct((s, LANES), I32),
                   jax.ShapeDtypeStruct((s, LANES), F32),
                   jax.ShapeDtypeStruct((SUBLANES, LANES), I32)),
        grid=(s // t,),
        in_specs=[row],
        out_specs=(row, row, row, pl.BlockSpec((SUBLANES, LANES), lambda i: (0, 0))),
        scratch_shapes=[pltpu.VMEM((SUBLANES, LANES), F32)],
        compiler_params=pltpu.CompilerParams(dimension_semantics=("arbitrary",),
                                             vmem_limit_bytes=VMEM_LIMIT),
        name="route",
    )(logits)


DISP_T = 512


def _dispatch_body(pstart_ref, cnt_ref, nused_ref, dest_ref, u_ref, xs_ref, zbuf_ref, sem, zsem):
    t = DISP_T
    tm = MOE_TM
    nb = xs_ref.shape[0] // tm

    @pl.when(pl.program_id(0) == 0)
    def _():
        zbuf_ref[...] = jnp.zeros_like(zbuf_ref)

        def pad_copy(e, r):
            return pltpu.make_async_copy(zbuf_ref.at[pl.ds(0, 1), :],
                                         xs_ref.at[pl.ds(pstart_ref[e] + cnt_ref[e] + r, 1), :], zsem)

        def blk_copy(b):
            return pltpu.make_async_copy(zbuf_ref, xs_ref.at[pl.ds(pl.multiple_of(b * tm, tm), tm), :],
                                         zsem)

        def per_expert(fn):
            def body(e, _):
                n_pad = (-cnt_ref[e]) & (tm - 1)
                lax.fori_loop(0, n_pad, lambda r, _: (fn(pad_copy(e, r)), 0)[1], 0)
                return 0
            lax.fori_loop(0, N_EXPERTS, body, 0)

        per_expert(lambda cp: cp.start())
        lax.fori_loop(nused_ref[0], nb, lambda b, _: (blk_copy(b).start(), 0)[1], 0)
        per_expert(lambda cp: cp.wait())
        lax.fori_loop(nused_ref[0], nb, lambda b, _: (blk_copy(b).wait(), 0)[1], 0)

    def copy(r, k):
        dest = dest_ref[0, 0, r * TOP_K + k]
        return pltpu.make_async_copy(u_ref.at[pl.ds(r, 1), :], xs_ref.at[pl.ds(dest, 1), :], sem)

    def start(r, _):
        for k in range(TOP_K):
            copy(r, k).start(priority=k % 2)
        return 0

    def wait(r, _):
        for k in range(TOP_K):
            copy(r, k).wait()
        return 0

    lax.fori_loop(0, t, start, 0, unroll=2)
    lax.fori_loop(0, t, wait, 0, unroll=8)


def _dispatch(pstart, counts, n_used, dest3, u2, n_rows):
    s, d = u2.shape
    t = DISP_T
    smem = lambda: pl.BlockSpec((1, 1, t * TOP_K), lambda i, *_: (i, 0, 0),
                                memory_space=pltpu.SMEM)
    return pl.pallas_call(
        _dispatch_body,
        out_shape=jax.ShapeDtypeStruct((n_rows, d), F32),
        grid_spec=pltpu.PrefetchScalarGridSpec(
            num_scalar_prefetch=3,
            grid=(s // t,),
            in_specs=[smem(), pl.BlockSpec((t, d), lambda i, *_: (i, 0))],
            out_specs=pl.BlockSpec(memory_space=pl.ANY),
            scratch_shapes=[pltpu.VMEM((MOE_TM, d), F32), pltpu.SemaphoreType.DMA,
                            pltpu.SemaphoreType.DMA]),
        compiler_params=pltpu.CompilerParams(dimension_semantics=("arbitrary",),
                                             vmem_limit_bytes=VMEM_LIMIT),
        name="dispatch",
    )(pstart, counts, n_used, dest3, u2)


PERM_W = 2 * LANES


def _experts_body(be_ref, nv_ref, nu_ref, eo_ref, nx_ref, x_ref, wgu_hbm, bgu_ref, wd_hbm, bd_ref,
                  perm_ref, o_ref, wgu_f, wd_f, wgu_s, wd_s, sems):
    b = pl.program_id(0)
    nvalid = nv_ref[b]
    e = be_ref[b]
    prev = be_ref[jnp.maximum(b - 1, 0)]
    changed = jnp.logical_or(b == 0, e != prev)
    f = wgu_s.shape[-1]

    def fetch(expert, slot):
        return (pltpu.make_async_copy(wgu_hbm.at[expert], wgu_f.at[slot], sems.at[slot]),
                pltpu.make_async_copy(wd_hbm.at[expert], wd_f.at[slot], sems.at[slot]))

    @pl.when(jnp.logical_and(changed, nvalid > 0))
    def _():
        slot = eo_ref[b] & 1

        @pl.when(b == 0)
        def _():
            for cp in fetch(e, slot):
                cp.start()

        for cp in fetch(e, slot):
            cp.wait()
        nxt = nx_ref[e]

        @pl.when(nxt >= 0)
        def _():
            for cp in fetch(nxt, 1 - slot):
                cp.start()

        for g in range(f // PERM_W):
            cols = slice(g * PERM_W, (g + 1) * PERM_W)
            wgu_s[:, cols] = jnp.dot(wgu_f[slot, :, cols].astype(BF16), perm_ref[...],
                                     preferred_element_type=F32).astype(BF16)
        wd_s[...] = wd_f[slot].astype(BF16)

    @pl.when(nvalid == 0)
    def _():
        o_ref[...] = jnp.zeros_like(o_ref)

    @pl.when(nvalid > 0)
    def _():
        x = x_ref[...].astype(BF16)
        gu = jnp.dot(x, wgu_s[...], preferred_element_type=F32) + bgu_ref[...]
        acts = []
        for g in range(f // PERM_W):
            glu = jnp.minimum(gu[:, g * PERM_W:g * PERM_W + LANES], SWIGLU_LIMIT)
            lin = jnp.clip(gu[:, g * PERM_W + LANES:(g + 1) * PERM_W], -SWIGLU_LIMIT, SWIGLU_LIMIT)
            acts.append((glu * _sigmoid(SWIGLU_ALPHA * glu) * (lin + 1.0)).astype(BF16))
        act = jnp.concatenate(acts, axis=-1)
        o_ref[...] = jnp.dot(act, wd_s[...], preferred_element_type=F32) + bd_ref[...]


def _experts(block_expert, block_nvalid, n_used, expert_ord, next_used, xs, w_gate_up, bgu_perm,
             w_down, b_down3, perm):
    n_rows, d = xs.shape
    tm = MOE_TM
    nb = n_rows // tm
    e, _, f = w_gate_up.shape
    de = w_down.shape[1]
    blk = lambda b, be, nv, nu, *_: (jnp.minimum(b, jnp.maximum(nu[0] - 1, 0)), 0)
    by_expert = lambda b, be, *_: (be[b], 0, 0)
    return pl.pallas_call(
        _experts_body,
        out_shape=jax.ShapeDtypeStruct((n_rows, d), F32),
        grid_spec=pltpu.PrefetchScalarGridSpec(
            num_scalar_prefetch=5,
            grid=(nb,),
            in_specs=[pl.BlockSpec((tm, d), blk),
                      pl.BlockSpec(memory_space=pl.ANY),
                      pl.BlockSpec((None, 1, f), by_expert),
                      pl.BlockSpec(memory_space=pl.ANY),
                      pl.BlockSpec((None, 1, d), by_expert),
                      pl.BlockSpec((PERM_W, PERM_W), lambda b, *_: (0, 0))],
            out_specs=pl.BlockSpec((tm, d), lambda b, *_: (b, 0)),
            scratch_shapes=[pltpu.VMEM((2, d, f), F32), pltpu.VMEM((2, de, d), F32),
                            pltpu.VMEM((d, f), BF16), pltpu.VMEM((de, d), BF16),
                            pltpu.SemaphoreType.DMA((2,))]),
        compiler_params=pltpu.CompilerParams(dimension_semantics=("arbitrary",),
                                             vmem_limit_bytes=VMEM_LIMIT),
        name="experts",
    )(block_expert, block_nvalid, n_used, expert_ord, next_used, xs, w_gate_up, bgu_perm, w_down,
      b_down3, perm)


COMB_T = 256


def _combine_body(alpha, cur_ref, nxt_ref, x1_ref, w_ref, gate_ref, lng_ref, lnb_ref,
                  ys_ref, o_ref, gbuf_ref, sems):
    t = COMB_T
    i = pl.program_id(0)
    slot = i & 1

    d = x1_ref.shape[1]

    def copy(src_ref, s, r, k):
        return pltpu.make_async_copy(ys_ref.at[pl.ds(src_ref[0, 0, r * TOP_K + k], 1), :],
                                     gbuf_ref.at[s, pl.ds(r, 1), pl.ds(k * d, d)], sems.at[s])

    def issue(src_ref, s):
        def body(r, _):
            for k in range(TOP_K):
                copy(src_ref, s, r, k).start(priority=k % 2)
            return 0
        lax.fori_loop(0, t, body, 0, unroll=2)

    @pl.when(i == 0)
    def _():
        issue(cur_ref, slot)

    @pl.when(i + 1 < pl.num_programs(0))
    def _():
        issue(nxt_ref, 1 - slot)

    def wait(r, _):
        for k in range(TOP_K):
            copy(cur_ref, slot, r, k).wait()
        return 0

    lax.fori_loop(0, t, wait, 0, unroll=8)
    w = w_ref[...]
    y = w[:, 0:1] * gbuf_ref[slot, :, 0:d]
    for k in range(1, TOP_K):
        y = y + w[:, k:k + 1] * gbuf_ref[slot, :, k * d:(k + 1) * d]
    r = alpha * x1_ref[...] + gate_ref[0:1, :] * y
    o_ref[...] = _layer_norm_rows(r, lng_ref[...], lnb_ref[...])


def _combine(alpha, dest3, x1, w_top, mods, lng, lnb, ys):
    s, d = x1.shape
    t = COMB_T
    n = s // t
    return pl.pallas_call(
        functools.partial(_combine_body, alpha),
        out_shape=jax.ShapeDtypeStruct((s, d), F32),
        grid=(n,),
        in_specs=[pl.BlockSpec((1, 1, t * TOP_K), lambda i: (i, 0, 0), memory_space=pltpu.SMEM),
                  pl.BlockSpec((1, 1, t * TOP_K), lambda i: (jnp.minimum(i + 1, n - 1), 0, 0),
                               memory_space=pltpu.SMEM),
                  pl.BlockSpec((t, d), lambda i: (i, 0)),
                  pl.BlockSpec((t, LANES), lambda i: (i, 0)),
                  pl.BlockSpec((SUBLANES, d), lambda i: (0, 5)),
                  pl.BlockSpec((1, d), lambda i: (0, 0)),
                  pl.BlockSpec((1, d), lambda i: (0, 0)),
                  pl.BlockSpec(memory_space=pl.ANY)],
        out_specs=pl.BlockSpec((t, d), lambda i: (i, 0)),
        scratch_shapes=[pltpu.VMEM((2, t, TOP_K * d), F32), pltpu.SemaphoreType.DMA((2,))],
        compiler_params=pltpu.CompilerParams(dimension_semantics=("arbitrary",),
                                             vmem_limit_bytes=VMEM_LIMIT),
        name="combine",
    )(dest3, dest3, x1, w_top, mods, lng, lnb, ys)


def _perm_matrix():
    p = np.zeros((PERM_W, PERM_W), np.float32)
    for c in range(PERM_W):
        p[c, (c // 2) + (LANES if c % 2 else 0)] = 1.0
    return jnp.asarray(p, BF16)


def _pad_lanes(v, offset):
    return jnp.zeros((1, LANES), F32).at[0, offset:offset + v.shape[0]].set(v.astype(F32))


def _layer(x2, silu_in, positions, layer, w_ada, b_ada, w_in, conv_w, dn_a_log, dn_dt_bias,
           dn_norm_w, w_dn_proj, lambda_q1, lambda_k1, lambda_q2, lambda_k2, da_norm_w,
           w_da_proj, w_o, ln1_g, ln1_b, w_router, b_router, w_gate_up, b_gate_up, w_down,
           b_down, ln2_g, ln2_b, alpha):
    s, d = x2.shape
    mods = _mods(silu_in, w_ada, b_ada[None, :])

    offs = np.cumsum([0, DN_WIDTH, DN_WIDTH, DN_WIDTH, DN_WIDTH, DN_HEADS, DN_HEADS,
                      DA_WIDTH, DA_WIDTH, DA_WIDTH, d, d])
    span = lambda a, b: w_in[:, offs[a]:offs[b]].astype(BF16)
    w_re = jnp.concatenate(
        [span(9, 11), span(0, 4), span(6, 9), span(4, 6),
         jnp.zeros((d, NP_COLS - COL_BA - 2 * DN_HEADS), BF16)], axis=1)
    proj = _inproj(x2, mods, w_re)

    half = DA_HEAD_DIM // 2
    inv_freq = ROPE_THETA ** (-jnp.arange(half, dtype=F32) / half)
    invf = jnp.tile(inv_freq, LANES // half)[None, :]
    sgn = jnp.tile(jnp.concatenate([-jnp.ones((half,), F32), jnp.ones((half,), F32)]),
                   LANES // DA_HEAD_DIM)[None, :]
    qt4, k12, vt3 = _attnprep(positions.reshape(s, 1), invf, sgn, proj)
    lam_init = 0.8 - 0.6 * math.exp(-0.3 * layer)
    o_da = _attn(lam_init, qt4, k12, vt3, lambda_q1[None, :], lambda_k1[None, :],
                 lambda_q2[None, :], lambda_k2[None, :], da_norm_w[:, None])

    o_dn = _deltanet(proj, conv_w, _pad_lanes(dn_a_log, DN_HEADS), _pad_lanes(dn_dt_bias, DN_HEADS))

    wr_pad = jnp.zeros((d, LANES), BF16).at[:, :N_EXPERTS].set(w_router.astype(BF16))
    x1, u2, logits = _post(alpha, x2, o_dn, proj, o_da, mods, dn_norm_w[None, :],
                           w_dn_proj.astype(BF16), w_da_proj.astype(BF16), w_o.astype(BF16),
                           ln1_g[None, :], ln1_b[None, :], wr_pad, _pad_lanes(b_router, 0))

    idx_l, rank_l, w_top, counts_l = _route(logits)
    counts = counts_l[0, :N_EXPERTS]
    tm = MOE_TM
    nblk_e = (counts + tm - 1) // tm
    blk_end = jnp.cumsum(nblk_e)
    blk_start = blk_end - nblk_e
    pstart = (blk_start * tm).astype(I32)
    nb = (s * TOP_K) // tm + N_EXPERTS
    n_used = blk_end[-1]
    bids = jnp.arange(nb, dtype=I32)
    last_e = jnp.max(jnp.where(counts > 0, jnp.arange(N_EXPERTS, dtype=I32), 0))
    be = jnp.sum((bids[:, None] >= blk_end[None, :]).astype(I32), axis=1)
    be = jnp.where(bids < n_used, jnp.minimum(be, N_EXPERTS - 1), last_e).astype(I32)
    nvalid = jnp.clip(counts[be] - (bids - blk_start[be]) * tm, 0, tm)
    nvalid = jnp.where(bids < n_used, nvalid, 0).astype(I32)
    e_ids = jnp.arange(N_EXPERTS, dtype=I32)
    idx4 = idx_l[:, :TOP_K]
    dest = jnp.sum(jnp.where(idx4[:, :, None] == e_ids, pstart, 0), axis=-1) + rank_l[:, :TOP_K]
    dest = dest.astype(I32)

    n_used = n_used.astype(I32)[None]
    xs = _dispatch(pstart, counts.astype(I32), n_used,
                   dest.reshape(s // DISP_T, 1, DISP_T * TOP_K), u2, nb * tm)
    e, _, f = w_gate_up.shape
    bgu_perm = b_gate_up.reshape(e, f // PERM_W, LANES, 2).transpose(0, 1, 3, 2).reshape(e, 1, f)
    used = counts > 0
    expert_ord = (jnp.cumsum(used.astype(I32)) - 1)[be].astype(I32)
    later_used = used[None, :] & (e_ids[None, :] > e_ids[:, None])
    next_used = jnp.min(jnp.where(later_used, e_ids[None, :], N_EXPERTS), axis=1)
    next_used = jnp.where(next_used == N_EXPERTS, -1, next_used).astype(I32)
    ys = _experts(be, nvalid, n_used, expert_ord, next_used, xs, w_gate_up, bgu_perm, w_down,
                  b_down[:, None, :], _perm_matrix())
    return _combine(alpha, dest.reshape(s // COMB_T, 1, COMB_T * TOP_K), x1, w_top, mods,
                    ln2_g[None, :], ln2_b[None, :], ys)


def kernel(x, c, positions, w_ada, b_ada, w_in, conv_w, dn_a_log, dn_dt_bias, dn_norm_w, w_dn_proj, lambda_q1, lambda_k1, lambda_q2, lambda_k2, da_norm_w, w_da_proj, w_o, ln1_g, ln1_b, w_router, b_router, w_gate_up, b_gate_up, w_down, b_down, ln2_g, ln2_b):
    b, s, d = x.shape
    assert b == 1, "kernel is written for a single sequence"
    depth = w_ada.shape[0]
    alpha = (2.0 * depth) ** 0.25
    c8 = jnp.zeros((SUBLANES, d), F32).at[0:1, :].set(c.astype(F32))
    x2 = x.reshape(s, d)
    for l in range(depth):
        x2 = _layer(x2, c8, positions, l, w_ada[l], b_ada[l], w_in[l], conv_w[l], dn_a_log[l],
                    dn_dt_bias[l], dn_norm_w[l], w_dn_proj[l], lambda_q1[l], lambda_k1[l],
                    lambda_q2[l], lambda_k2[l], da_norm_w[l], w_da_proj[l], w_o[l], ln1_g[l],
                    ln1_b[l], w_router[l], b_router[l], w_gate_up[l], b_gate_up[l], w_down[l],
                    b_down[l], ln2_g[l], ln2_b[l], alpha)
    return x2.reshape(b, s, d)
```
